```python
import jax, jax.numpy as jnp
from jax import lax
import numpy as np

D_MODEL = 1024
BATCH = 1
SEQ = 16384
DEPTH = 4
DEC_BATCH = 8
DEC_SEQ = 16
PAST_LEN = 1024

CHUNK = 64
D_A = 256
HEADS_A = 4
D_B = 512
HEADS_B = 8
HEAD_DIM_B = D_B // HEADS_B
D_C = 256
HEADS_C = 4
D_MIX = D_A + D_B + D_C
D_IN = 3 * D_A + 2 * D_B + 2 * D_C
CONV_A = 3
CONV_B = 4
CONV_C = 31
RG_C = 8.0
D_FF = 2816
N_SUB = 3
EPS = 1e-6

kernel_name = 'hybrid_streaming_encoder_step'


def rmsnorm(x, g):
    xf = x.astype(jnp.float32)
    y = xf * lax.rsqrt(jnp.mean(xf * xf, axis=-1, keepdims=True) + EPS)
    return (y * g.astype(jnp.float32)).astype(x.dtype)


def layernorm(x, g, b):
    xf = x.astype(jnp.float32)
    mu = jnp.mean(xf, axis=-1, keepdims=True)
    var = jnp.mean(jnp.square(xf - mu), axis=-1, keepdims=True)
    y = (xf - mu) * lax.rsqrt(var + EPS)
    return (y * g.astype(jnp.float32) + b.astype(jnp.float32)).astype(x.dtype)


def causal_dwconv(u, buf, w):
    k = w.shape[0]
    xp = jnp.concatenate([buf.astype(u.dtype), u], axis=1)
    y = lax.conv_general_dilated(xp, w[:, None, :].astype(u.dtype), window_strides=(1,),
                                 padding='VALID', dimension_numbers=('NWC', 'WIO', 'NWC'),
                                 feature_group_count=u.shape[-1])
    return y, xp[:, xp.shape[1] - (k - 1):]


def swiglu(h, w1, w3, w2):
    return (jax.nn.silu(h @ w1) * (h @ w3)) @ w2


def _lin_combine(left, right):
    a_l, b_l = left
    a_r, b_r = right
    return a_l * a_r, a_r * b_l + b_r


def rglru(xb, h0, w_r, b_r, w_i, b_i, lam):
    bsz, t, _ = xb.shape
    xh = xb.reshape(bsz, t, HEADS_B, HEAD_DIM_B)
    r = jax.nn.sigmoid((jnp.einsum('bthi,hij->bthj', xh, w_r).reshape(bsz, t, D_B) + b_r).astype(jnp.float32))
    i = jax.nn.sigmoid((jnp.einsum('bthi,hij->bthj', xh, w_i).reshape(bsz, t, D_B) + b_i).astype(jnp.float32))
    log_a = RG_C * r * jax.nn.log_sigmoid(lam.astype(jnp.float32))
    a = jnp.exp(log_a)
    bterm = jnp.sqrt(-jnp.expm1(2.0 * log_a)) * i * xb.astype(jnp.float32)
    bterm = bterm.at[:, 0].add(a[:, 0] * h0.astype(jnp.float32))
    _, h = lax.associative_scan(_lin_combine, (a, bterm), axis=1)
    return h.astype(xb.dtype), h[:, -1].astype(h0.dtype)


def token_mixer(h, st_a, st_b, st_h, st_c, w_in, w_out, w_conv_a, w_conv_b, b_conv_b,
                w_gate_r, b_gate_r, w_gate_i, b_gate_i, rg_lambda, w_conv_c, b_conv_c,
                ln_c_g, ln_c_b, grp_g):
    u = h @ w_in
    splits = [D_A, 2 * D_A, 3 * D_A, 3 * D_A + D_B, 3 * D_A + 2 * D_B, 3 * D_A + 2 * D_B + D_C]
    a_b, a_c, a_x, b_x, b_g, c_v, c_g = jnp.split(u, splits, axis=-1)
    ya, new_a = causal_dwconv(a_c * a_x, st_a, w_conv_a)
    ya = a_b * ya
    xb, new_b = causal_dwconv(b_x, st_b, w_conv_b)
    yb, new_h = rglru(xb + b_conv_b, st_h, w_gate_r, b_gate_r, w_gate_i, b_gate_i, rg_lambda)
    yb = yb * jax.nn.gelu(b_g)
    v = c_v * jax.nn.sigmoid(c_g)
    yc, new_c = causal_dwconv(v, st_c, w_conv_c)
    yc = jax.nn.silu(layernorm(yc + b_conv_c, ln_c_g, ln_c_b))
    y = jnp.concatenate([rmsnorm(ya, grp_g[:D_A]),
                         rmsnorm(yb, grp_g[D_A:D_A + D_B]),
                         rmsnorm(yc, grp_g[D_A + D_B:])], axis=-1)
    return y @ w_out, new_a, new_b, new_h, new_c


def setup_inputs(seed: int = 0) -> dict:
    key = jax.random.key(seed)
    ks = jax.random.split(key, 40)
    f32 = jnp.float32

    def nrm(k, shape, scale):
        return jax.random.normal(k, shape, f32) * scale

    s = D_MODEL ** -0.5
    u_lam = jax.random.uniform(ks[21], (DEPTH, D_B), f32, 0.9, 0.999)
    return {
        'x_prompt': nrm(ks[0], (BATCH, SEQ, D_MODEL), 1.0),
        'x_sample': nrm(ks[1], (DEC_BATCH, DEC_SEQ, D_MODEL), 1.0),
        'state_conv_a': nrm(ks[2], (DEPTH, DEC_BATCH, CONV_A - 1, D_A), 1.0),
        'state_conv_b': nrm(ks[3], (DEPTH, DEC_BATCH, CONV_B - 1, D_B), 1.0),
        'state_rglru': nrm(ks[4], (DEPTH, DEC_BATCH, D_B), 0.5),
        'state_conv_c': nrm(ks[5], (DEPTH, DEC_BATCH, CONV_C - 1, D_C), 1.0),
        'c_prompt': nrm(ks[6], (BATCH, D_MODEL), 1.0),
        'c_sample': nrm(ks[7], (DEC_BATCH, D_MODEL), 1.0),
        'w_ada': nrm(ks[8], (DEPTH, D_MODEL, 3 * N_SUB * D_MODEL), s),
        'b_ada': nrm(ks[9], (DEPTH, 3 * N_SUB * D_MODEL), 0.02),
        'norm_pre': 1.0 + nrm(ks[10], (DEPTH, N_SUB, D_MODEL), 0.02),
        'norm_post': 1.0 + nrm(ks[11], (DEPTH, N_SUB, D_MODEL), 0.02),
        'ffn_w1': nrm(ks[12], (DEPTH, 2, D_MODEL, D_FF), s),
        'ffn_w3': nrm(ks[13], (DEPTH, 2, D_MODEL, D_FF), s),
        'ffn_w2': nrm(ks[14], (DEPTH, 2, D_FF, D_MODEL), D_FF ** -0.5),
        'w_in': nrm(ks[15], (DEPTH, D_MODEL, D_IN), s),
        'w_out': nrm(ks[16], (DEPTH, D_MIX, D_MODEL), D_MIX ** -0.5),
        'w_conv_a': nrm(ks[17], (DEPTH, CONV_A, D_A), CONV_A ** -0.5),
        'w_conv_b': nrm(ks[18], (DEPTH, CONV_B, D_B), CONV_B ** -0.5),
        'b_conv_b': nrm(ks[19], (DEPTH, D_B), 0.02),
        'w_gate_r': nrm(ks[20], (DEPTH, HEADS_B, HEAD_DIM_B, HEAD_DIM_B), HEAD_DIM_B ** -0.5),
        'b_gate_r': nrm(ks[22], (DEPTH, D_B), 0.02),
        'w_gate_i': nrm(ks[23], (DEPTH, HEADS_B, HEAD_DIM_B, HEAD_DIM_B), HEAD_DIM_B ** -0.5),
        'b_gate_i': nrm(ks[24], (DEPTH, D_B), 0.02),
        'rg_lambda': jnp.log(u_lam) - jnp.log1p(-u_lam),
        'w_conv_c': nrm(ks[25], (DEPTH, CONV_C, D_C), CONV_C ** -0.5),
        'b_conv_c': nrm(ks[26], (DEPTH, D_C), 0.02),
        'ln_c_g': 1.0 + nrm(ks[27], (DEPTH, D_C), 0.02),
        'ln_c_b': nrm(ks[28], (DEPTH, D_C), 0.02),
        'grp_g': 1.0 + nrm(ks[29], (DEPTH, D_MIX), 0.02),
    }


def reference(x_prompt, x_sample, state_conv_a, state_conv_b, state_rglru, state_conv_c,
              c_prompt, c_sample, w_ada, b_ada, norm_pre, norm_post, ffn_w1, ffn_w3, ffn_w2,
              w_in, w_out, w_conv_a, w_conv_b, b_conv_b, w_gate_r, b_gate_r, w_gate_i, b_gate_i,
              rg_lambda, w_conv_c, b_conv_c, ln_c_g, ln_c_b, grp_g):

    def trunk(x, c, st_a, st_b, st_h, st_c):
        bsz = x.shape[0]
        new_a, new_b, new_h, new_c = [], [], [], []
        for l in range(DEPTH):
            mod = (jax.nn.silu(c) @ w_ada[l] + b_ada[l]).reshape(bsz, 3 * N_SUB, 1, D_MODEL).astype(x.dtype)
            h = rmsnorm(x, norm_pre[l, 0]) * (1.0 + mod[:, 1]) + mod[:, 0]
            f = swiglu(h, ffn_w1[l, 0], ffn_w3[l, 0], ffn_w2[l, 0])
            x = x + 0.5 * mod[:, 2] * rmsnorm(f, norm_post[l, 0])
            h = rmsnorm(x, norm_pre[l, 1]) * (1.0 + mod[:, 4]) + mod[:, 3]
            m, na, nb, nh, nc = token_mixer(h, st_a[l], st_b[l], st_h[l], st_c[l], w_in[l], w_out[l],
                                            w_conv_a[l], w_conv_b[l], b_conv_b[l], w_gate_r[l], b_gate_r[l],
                                            w_gate_i[l], b_gate_i[l], rg_lambda[l], w_conv_c[l], b_conv_c[l],
                                            ln_c_g[l], ln_c_b[l], grp_g[l])
            x = x + mod[:, 5] * rmsnorm(m, norm_post[l, 1])
            h = rmsnorm(x, norm_pre[l, 2]) * (1.0 + mod[:, 7]) + mod[:, 6]
            f = swiglu(h, ffn_w1[l, 1], ffn_w3[l, 1], ffn_w2[l, 1])
            x = x + 0.5 * mod[:, 8] * rmsnorm(f, norm_post[l, 2])
            new_a.append(na)
            new_b.append(nb)
            new_h.append(nh)
            new_c.append(nc)
        return x, jnp.stack(new_a), jnp.stack(new_b), jnp.stack(new_h), jnp.stack(new_c)

    dt = x_prompt.dtype
    p_a0 = jnp.zeros((DEPTH, BATCH, CONV_A - 1, D_A), dt)
    p_b0 = jnp.zeros((DEPTH, BATCH, CONV_B - 1, D_B), dt)
    p_h0 = jnp.zeros((DEPTH, BATCH, D_B), dt)
    p_c0 = jnp.zeros((DEPTH, BATCH, CONV_C - 1, D_C), dt)
    y_prompt, pa, pb, ph, pc = trunk(x_prompt, c_prompt, p_a0, p_b0, p_h0, p_c0)
    y_sample, sa, sb, sh, sc = trunk(x_sample, c_sample, state_conv_a, state_conv_b, state_rglru, state_conv_c)
    return (y_prompt, y_sample, pa, pb, ph, pc, sa, sb, sh, sc)
```

```python
import functools

import jax
import jax.numpy as jnp
from jax import lax
from jax.experimental import pallas as pl
from jax.experimental.pallas import tpu as pltpu

D_MODEL = 1024
DEPTH = 4
D_A = 256
D_B = 512
HEADS_B = 8
HEAD_DIM_B = D_B // HEADS_B
D_C = 256
D_MIX = D_A + D_B + D_C
D_IN = 3 * D_A + 2 * D_B + 2 * D_C
CONV_A = 3
CONV_B = 4
CONV_C = 31
RG_C = 8.0
D_FF = 2816
N_SUB = 3
EPS = 1e-6

SUBLANES = 8
MXU_DIM = 256
SEG_LEN = 64
FF_CHUNK = MXU_DIM
CONV_ROWS = 64
VMEM_LIMIT = 56 * 1024 * 1024

O_AB, O_AC, O_AX = 0, D_A, 2 * D_A
O_BX = 3 * D_A
O_BG = O_BX + D_B
O_CV = O_BG + D_B
O_CG = O_CV + D_C

F32 = jnp.float32
BF16 = jnp.bfloat16


def _rms(x, g):
    ms = jnp.mean(x * x, axis=-1, keepdims=True)
    return x * lax.rsqrt(ms + EPS) * g


def _prenorm(x_ref, shift_ref, scale_ref, gpre_ref):
    x = x_ref[...]
    h = _rms(x, gpre_ref[...]) * (1.0 + scale_ref[...]) + shift_ref[...]
    return h.reshape(x.shape[0] * SUBLANES, x.shape[2]).astype(BF16)


def _ada_kernel(c_ref, w_ref, b_ref, o_ref):
    c = c_ref[...]
    s = (c * jax.nn.sigmoid(c)).astype(BF16)
    o_ref[...] = jnp.dot(s, w_ref[...].astype(BF16), preferred_element_type=F32) + b_ref[...]


def _ffn_kernel(x_ref, shift_ref, scale_ref, gate_ref, gpre_ref, gpost_ref,
                w1_ref, w3_ref, w2_ref, o_ref, act_scr):
    tl = x_ref.shape[0]
    hb = _prenorm(x_ref, shift_ref, scale_ref, gpre_ref)
    for c in range(D_FF // FF_CHUNK):
        cols = slice(c * FF_CHUNK, (c + 1) * FF_CHUNK)
        g = jnp.dot(hb, w1_ref[:, cols], preferred_element_type=F32)
        v = jnp.dot(hb, w3_ref[:, cols], preferred_element_type=F32)
        act_scr[:, cols] = (g * jax.nn.sigmoid(g) * v).astype(BF16)
    f = jnp.dot(act_scr[...], w2_ref[...], preferred_element_type=F32)
    fn = _rms(f, gpost_ref[...]).reshape(tl, SUBLANES, D_MODEL)
    o_ref[...] = x_ref[...] + (0.5 * gate_ref[...]) * fn


def _build_halo(e_ref, prev_ref, taps, seg_len, chained):
    n = (taps - 1) * SUBLANES
    prev = prev_ref[...]
    if chained:
        cur = e_ref[seg_len * SUBLANES:seg_len * SUBLANES + n, :]
        sub = lax.broadcasted_iota(jnp.int32, cur.shape, 0) & (SUBLANES - 1)
        halo = jnp.where(sub == 0, pltpu.roll(prev, n - (SUBLANES - 1), axis=0),
                         pltpu.roll(cur, 1, axis=0))
    else:
        halo = prev
    e_ref[0:n, :] = halo


def _short_conv(e_ref, w_ref, taps, rows):
    acc = e_ref[0:rows, :] * w_ref[0:1, :]
    for k in range(1, taps):
        acc = acc + e_ref[k * SUBLANES:k * SUBLANES + rows, :] * w_ref[k:k + 1, :]
    return acc


def _mixer_kernel(x_ref, shift_ref, scale_ref, gate_ref, gpre_ref, gpost_ref,
                  win_ref, wout_ref, wca_ref, wcb_ref, bcb_ref,
                  wr_ref, br_ref, wi_ref, bi_ref, lam_ref,
                  wcc_ref, bcc_ref, lng_ref, lnb_ref, grp_ref,
                  sa_ref, sb_ref, sh_ref, sc_ref,
                  o_ref, na_ref, nb_ref, nh_ref, nc_ref,
                  u_scr, ea, eb, ec, a_scr, b_scr, hl_scr, p_scr, c_scr, y_scr,
                  *, chained):
    seg_len = x_ref.shape[0]
    rows = seg_len * SUBLANES

    @pl.when(pl.program_id(0) == 0)
    def _():
        na_ref[...] = sa_ref[...]
        nb_ref[...] = sb_ref[...]
        nh_ref[...] = sh_ref[...]
        nc_ref[...] = sc_ref[...]

    hb = _prenorm(x_ref, shift_ref, scale_ref, gpre_ref)
    for c in range(D_IN // MXU_DIM):
        cols = slice(c * MXU_DIM, (c + 1) * MXU_DIM)
        u_scr[:, cols] = jnp.dot(hb, win_ref[:, cols], preferred_element_type=F32)

    na = (CONV_A - 1) * SUBLANES
    ea[na:, :] = u_scr[:, O_AC:O_AC + D_A] * u_scr[:, O_AX:O_AX + D_A]
    _build_halo(ea, na_ref, CONV_A, seg_len, chained)
    ya = u_scr[:, O_AB:O_AB + D_A] * _short_conv(ea, wca_ref, CONV_A, rows)
    na_ref[...] = ea[rows:rows + na, :]
    y_scr[:, 0:D_A] = _rms(ya, grp_ref[:, 0:D_A]).astype(BF16)

    nb = (CONV_B - 1) * SUBLANES
    eb[nb:, :] = u_scr[:, O_BX:O_BX + D_B]
    _build_halo(eb, nb_ref, CONV_B, seg_len, chained)
    xb = _short_conv(eb, wcb_ref, CONV_B, rows) + bcb_ref[...]
    nb_ref[...] = eb[rows:rows + nb, :]
    xbb = xb.astype(BF16)
    half = D_B // 2
    gr = jnp.concatenate(
        [jnp.dot(xbb[:, :half], wr_ref[0], preferred_element_type=F32),
         jnp.dot(xbb[:, half:], wr_ref[1], preferred_element_type=F32)], axis=-1)
    gi = jnp.concatenate(
        [jnp.dot(xbb[:, :half], wi_ref[0], preferred_element_type=F32),
         jnp.dot(xbb[:, half:], wi_ref[1], preferred_element_type=F32)], axis=-1)
    r = jax.nn.sigmoid(gr + br_ref[...])
    ig = jax.nn.sigmoid(gi + bi_ref[...])
    log_a = RG_C * r * jax.nn.log_sigmoid(lam_ref[...])
    a = jnp.exp(log_a)
    a_scr[...] = a
    b_scr[...] = jnp.sqrt(-jnp.tanh(log_a) * (a * a + 1.0)) * ig * xb

    def scan_body(j, carry):
        hl, p = carry
        r0 = pl.multiple_of(j * SUBLANES, SUBLANES)
        a = a_scr[pl.ds(r0, SUBLANES), :]
        hl = a * hl + b_scr[pl.ds(r0, SUBLANES), :]
        p = a * p
        hl_scr[pl.ds(r0, SUBLANES), :] = hl
        p_scr[pl.ds(r0, SUBLANES), :] = p
        return hl, p

    seg_b, seg_a = lax.fori_loop(
        0, seg_len, scan_body,
        (jnp.zeros((SUBLANES, D_B), F32), jnp.ones((SUBLANES, D_B), F32)), unroll=8)

    prev_end = nh_ref[...]
    if chained:
        sub = lax.broadcasted_iota(jnp.int32, (SUBLANES, D_B), 0)
        ca, cb = seg_a, seg_b
        for d in (1, 2, 4):
            a_sh = jnp.where(sub >= d, pltpu.roll(ca, d, axis=0), 1.0)
            b_sh = jnp.where(sub >= d, pltpu.roll(cb, d, axis=0), 0.0)
            cb = ca * b_sh + cb
            ca = ca * a_sh
        c0 = jnp.broadcast_to(prev_end[SUBLANES - 1:SUBLANES, :], (SUBLANES, D_B))
        seg_end = ca * c0 + cb
        seg_start = jnp.where(sub == 0, c0, pltpu.roll(seg_end, 1, axis=0))
    else:
        seg_start = prev_end
        seg_end = seg_a * seg_start + seg_b
    nh_ref[...] = seg_end

    hl3 = hl_scr[...].reshape(seg_len, SUBLANES, D_B)
    p3 = p_scr[...].reshape(seg_len, SUBLANES, D_B)
    hseq = (hl3 + p3 * seg_start).reshape(rows, D_B)
    yb = hseq * jax.nn.gelu(u_scr[:, O_BG:O_BG + D_B])
    y_scr[:, D_A:D_A + D_B] = _rms(yb, grp_ref[:, D_A:D_A + D_B]).astype(BF16)

    nc = (CONV_C - 1) * SUBLANES
    ec[nc:, :] = u_scr[:, O_CV:O_CV + D_C] * jax.nn.sigmoid(u_scr[:, O_CG:O_CG + D_C])
    _build_halo(ec, nc_ref, CONV_C, seg_len, chained)

    def conv_body(blk, carry):
        r0 = pl.multiple_of(blk * CONV_ROWS, CONV_ROWS)
        acc = ec[pl.ds(r0, CONV_ROWS), :] * wcc_ref[0:1, :]
        for k in range(1, CONV_C):
            acc = acc + ec[pl.ds(r0 + k * SUBLANES, CONV_ROWS), :] * wcc_ref[k:k + 1, :]
        c_scr[pl.ds(r0, CONV_ROWS), :] = acc
        return carry

    lax.fori_loop(0, rows // CONV_ROWS, conv_body, 0)
    nc_ref[...] = ec[rows:rows + nc, :]
    yc = c_scr[...] + bcc_ref[...]
    mu = jnp.mean(yc, axis=-1, keepdims=True)
    var = jnp.mean(jnp.square(yc - mu), axis=-1, keepdims=True)
    yc = (yc - mu) * lax.rsqrt(var + EPS) * lng_ref[...] + lnb_ref[...]
    yc = yc * jax.nn.sigmoid(yc)
    y_scr[:, D_A + D_B:] = _rms(yc, grp_ref[:, D_A + D_B:]).astype(BF16)

    m = jnp.dot(y_scr[...], wout_ref[...], preferred_element_type=F32)
    mn = _rms(m, gpost_ref[...]).reshape(seg_len, SUBLANES, D_MODEL)
    o_ref[...] = x_ref[...] + gate_ref[...] * mn


def _const_spec(shape, index):
    return pl.BlockSpec(shape, lambda i: index, pipeline_mode=pl.Buffered(1))


def _mod_specs(layer, sub):
    return [_const_spec((None, SUBLANES, D_MODEL), (layer, 0, 3 * sub + k)) for k in range(3)]


def _ffn_call(x, mod, gpre, gpost, w1, w3, w2, layer, sub, which, tl):
    groups = x.shape[0]
    rows = tl * SUBLANES
    x_spec = pl.BlockSpec((tl, SUBLANES, D_MODEL), lambda i: (i, 0, 0))
    return pl.pallas_call(
        _ffn_kernel,
        grid=(groups // tl,),
        in_specs=[x_spec] + _mod_specs(layer, sub) + [
            _const_spec((None, 1, D_MODEL), (layer * N_SUB + sub, 0, 0)),
            _const_spec((None, 1, D_MODEL), (layer * N_SUB + sub, 0, 0)),
            _const_spec((None, None, D_MODEL, D_FF), (layer, which, 0, 0)),
            _const_spec((None, None, D_MODEL, D_FF), (layer, which, 0, 0)),
            _const_spec((None, None, D_FF, D_MODEL), (layer, which, 0, 0)),
        ],
        out_specs=x_spec,
        out_shape=jax.ShapeDtypeStruct(x.shape, F32),
        scratch_shapes=[pltpu.VMEM((rows, D_FF), BF16)],
        compiler_params=pltpu.CompilerParams(
            dimension_semantics=("parallel",), vmem_limit_bytes=VMEM_LIMIT),
        name=f"ffn_l{layer}_s{sub}",
    )(x, mod, mod, mod, gpre, gpost, w1, w3, w2)


def _mixer_call(x, mod, gpre, gpost, p, states, layer, seg_len, chained):
    groups = x.shape[0]
    rows = seg_len * SUBLANES
    sub = 1
    x_spec = pl.BlockSpec((seg_len, SUBLANES, D_MODEL), lambda i: (i, 0, 0))

    def lspec(arr):
        shape = (None,) + arr.shape[1:]
        return _const_spec(shape, (layer,) + (0,) * (arr.ndim - 1))

    def sspec(arr):
        return pl.BlockSpec(arr.shape[1:], lambda i: (0, 0))

    weights = [p["w_in"], p["w_out"], p["w_conv_a"], p["w_conv_b"], p["b_conv_b"],
               p["w_gate_r"], p["b_gate_r"], p["w_gate_i"], p["b_gate_i"], p["rg_lambda"],
               p["w_conv_c"], p["b_conv_c"], p["ln_c_g"], p["ln_c_b"], p["grp_g"]]
    state_shapes = [jax.ShapeDtypeStruct(s.shape[1:], F32) for s in states]
    outs = pl.pallas_call(
        functools.partial(_mixer_kernel, chained=chained),
        grid=(groups // seg_len,),
        in_specs=[x_spec] + _mod_specs(layer, sub) + [
            _const_spec((None, 1, D_MODEL), (layer * N_SUB + sub, 0, 0)),
            _const_spec((None, 1, D_MODEL), (layer * N_SUB + sub, 0, 0)),
        ] + [lspec(w) for w in weights] + [lspec(s) for s in states],
        out_specs=[x_spec] + [sspec(s) for s in states],
        out_shape=[jax.ShapeDtypeStruct(x.shape, F32)] + state_shapes,
        scratch_shapes=[
            pltpu.VMEM((rows, D_IN), F32),
            pltpu.VMEM((rows + (CONV_A - 1) * SUBLANES, D_A), F32),
            pltpu.VMEM((rows + (CONV_B - 1) * SUBLANES, D_B), F32),
            pltpu.VMEM((rows + (CONV_C - 1) * SUBLANES, D_C), F32),
            pltpu.VMEM((rows, D_B), F32),
            pltpu.VMEM((rows, D_B), F32),
            pltpu.VMEM((rows, D_B), F32),
            pltpu.VMEM((rows, D_B), F32),
            pltpu.VMEM((rows, D_C), F32),
            pltpu.VMEM((rows, D_MIX), BF16),
        ],
        compiler_params=pltpu.CompilerParams(
            dimension_semantics=("arbitrary",), vmem_limit_bytes=VMEM_LIMIT),
        name=f"mixer_l{layer}",
    )(x, mod, mod, mod, gpre, gpost, *weights, *states)
    return outs[0], outs[1:]


def _block_diag_halves(w):
    heads_per_half = MXU_DIM // HEAD_DIM_B
    w = w.reshape(DEPTH, 2, heads_per_half, HEAD_DIM_B, HEAD_DIM_B)
    eye = jnp.eye(heads_per_half, dtype=w.dtype)
    bd = jnp.einsum("lhaij,ab->lhaibj", w, eye)
    return bd.reshape(DEPTH, 2, MXU_DIM, MXU_DIM).astype(BF16)


def _trunk(x, mod, params, states, seg_len, chained):
    new_states = []
    for l in range(DEPTH):
        x = _ffn_call(x, mod, params["norm_pre"], params["norm_post"], params["ffn_w1"],
                      params["ffn_w3"], params["ffn_w2"], l, 0, 0, seg_len)
        x, ns = _mixer_call(x, mod, params["norm_pre"], params["norm_post"], params, states,
                            l, seg_len, chained)
        new_states.append(ns)
        x = _ffn_call(x, mod, params["norm_pre"], params["norm_post"], params["ffn_w1"],
                      params["ffn_w3"], params["ffn_w2"], l, 2, 1, seg_len)
    return x, [jnp.stack([ns[k] for ns in new_states]) for k in range(4)]


def kernel(x_prompt, x_sample, state_conv_a, state_conv_b, state_rglru, state_conv_c, c_prompt, c_sample, w_ada, b_ada, norm_pre, norm_post, ffn_w1, ffn_w3, ffn_w2, w_in, w_out, w_conv_a, w_conv_b, b_conv_b, w_gate_r, b_gate_r, w_gate_i, b_gate_i, rg_lambda, w_conv_c, b_conv_c, ln_c_g, ln_c_b, grp_g):
    batch, seq, _ = x_prompt.shape
    dec_batch, dec_seq, _ = x_sample.shape
    assert batch == 1 and dec_batch == SUBLANES
    assert seq % (SUBLANES * SEG_LEN) == 0 and SEG_LEN >= CONV_C - 1

    n_mod = 3 * N_SUB * D_MODEL
    c_all = jnp.concatenate(
        [c_prompt, c_sample, jnp.zeros((2 * SUBLANES - batch - dec_batch, D_MODEL), F32)], axis=0)
    mod_all = pl.pallas_call(
        _ada_kernel,
        grid=(DEPTH, n_mod // D_MODEL),
        in_specs=[pl.BlockSpec((2 * SUBLANES, D_MODEL), lambda l, n: (0, 0)),
                  pl.BlockSpec((None, D_MODEL, D_MODEL), lambda l, n: (l, 0, n)),
                  pl.BlockSpec((None, 1, D_MODEL), lambda l, n: (l, 0, n))],
        out_specs=pl.BlockSpec((None, 2 * SUBLANES, D_MODEL), lambda l, n: (l, 0, n)),
        out_shape=jax.ShapeDtypeStruct((DEPTH, 2 * SUBLANES, n_mod), F32),
        compiler_params=pltpu.CompilerParams(
            dimension_semantics=("parallel", "parallel"), vmem_limit_bytes=VMEM_LIMIT),
        name="adaln",
    )(c_all, w_ada, b_ada.reshape(DEPTH, 1, n_mod))
    mod_p = jnp.broadcast_to(mod_all[:, 0:1], (DEPTH, SUBLANES, n_mod))
    mod_s = mod_all[:, 1:1 + SUBLANES]

    def row(v):
        return v.reshape(DEPTH, 1, v.shape[-1])

    params = {
        "norm_pre": norm_pre.reshape(DEPTH * N_SUB, 1, D_MODEL),
        "norm_post": norm_post.reshape(DEPTH * N_SUB, 1, D_MODEL),
        "ffn_w1": ffn_w1.astype(BF16), "ffn_w3": ffn_w3.astype(BF16), "ffn_w2": ffn_w2.astype(BF16),
        "w_in": w_in.astype(BF16), "w_out": w_out.astype(BF16),
        "w_conv_a": w_conv_a, "w_conv_b": w_conv_b, "b_conv_b": row(b_conv_b),
        "w_gate_r": _block_diag_halves(w_gate_r), "b_gate_r": row(b_gate_r),
        "w_gate_i": _block_diag_halves(w_gate_i), "b_gate_i": row(b_gate_i),
        "rg_lambda": row(rg_lambda),
        "w_conv_c": w_conv_c, "b_conv_c": row(b_conv_c),
        "ln_c_g": row(ln_c_g), "ln_c_b": row(ln_c_b), "grp_g": row(grp_g),
    }

    tiles = seq // (SUBLANES * SEG_LEN)
    xp = x_prompt.reshape(tiles, SUBLANES, SEG_LEN, D_MODEL).transpose(0, 2, 1, 3)
    xp = xp.reshape(tiles * SEG_LEN, SUBLANES, D_MODEL)
    zero_states = [jnp.zeros((DEPTH, (CONV_A - 1) * SUBLANES, D_A), F32),
                   jnp.zeros((DEPTH, (CONV_B - 1) * SUBLANES, D_B), F32),
                   jnp.zeros((DEPTH, SUBLANES, D_B), F32),
                   jnp.zeros((DEPTH, (CONV_C - 1) * SUBLANES, D_C), F32)]
    yp, (pa, pb, ph, pc) = _trunk(xp, mod_p, params, zero_states, SEG_LEN, True)
    y_prompt = yp.reshape(tiles, SEG_LEN, SUBLANES, D_MODEL).transpose(0, 2, 1, 3)
    y_prompt = y_prompt.reshape(batch, seq, D_MODEL)

    def last_segment(s, taps_m1):
        return s.reshape(DEPTH, taps_m1, SUBLANES, s.shape[-1])[:, :, SUBLANES - 1][:, None]

    pa = last_segment(pa, CONV_A - 1)
    pb = last_segment(pb, CONV_B - 1)
    pc = last_segment(pc, CONV_C - 1)
    ph = ph[:, SUBLANES - 1:SUBLANES]

    def to_rows(s):
        return s.transpose(0, 2, 1, 3).reshape(DEPTH, s.shape[2] * SUBLANES, s.shape[3])

    def from_rows(s, taps_m1):
        return s.reshape(DEPTH, taps_m1, SUBLANES, s.shape[-1]).transpose(0, 2, 1, 3)

    xs = x_sample.transpose(1, 0, 2)
    s_states = [to_rows(state_conv_a), to_rows(state_conv_b), state_rglru, to_rows(state_conv_c)]
    ys, (sa, sb, sh, sc) = _trunk(xs, mod_s, params, s_states, dec_seq, False)
    y_sample = ys.transpose(1, 0, 2)
    sa = from_rows(sa, CONV_A - 1)
    sb = from_rows(sb, CONV_B - 1)
    sc = from_rows(sc, CONV_C - 1)

    return (y_prompt, y_sample, pa, pb, ph, pc, sa, sb, sh, sc)
```

```python
import functools

import jax
import jax.numpy as jnp
from jax import lax
from jax.experimental import pallas as pl
from jax.experimental.pallas import tpu as pltpu

D_MODEL = 1024
DEPTH = 4
D_A = 256
D_B = 512
HEADS_B = 8
HEAD_DIM_B = D_B // HEADS_B
D_C = 256
D_MIX = D_A + D_B + D_C
D_IN = 3 * D_A + 2 * D_B + 2 * D_C
CONV_A = 3
CONV_B = 4
CONV_C = 31
RG_C = 8.0
D_FF = 2816
N_SUB = 3
EPS = 1e-6

SUBLANES = 8
MXU_DIM = 256
SEG_LEN = 64
FF_CHUNK = MXU_DIM
CONV_ROWS = 64
ADA_K = MXU_DIM
FFN_GROUPS = 128
FFN_SUB_GROUPS = 32
VMEM_LIMIT = 56 * 1024 * 1024

O_AB, O_AC, O_AX = 0, D_A, 2 * D_A
O_BX = 3 * D_A
O_BG = O_BX + D_B
O_CV = O_BG + D_B
O_CG = O_CV + D_C

F32 = jnp.float32
BF16 = jnp.bfloat16


def _rms(x, g):
    ms = jnp.mean(x * x, axis=-1, keepdims=True)
    return x * lax.rsqrt(ms + EPS) * g


def _prenorm(x_ref, shift_ref, scale_ref, gpre_ref):
    x = x_ref[...]
    h = _rms(x, gpre_ref[...]) * (1.0 + scale_ref[...]) + shift_ref[...]
    return h.reshape(x.shape[0] * SUBLANES, x.shape[2]).astype(BF16)


def _ada_kernel(c_ref, w_ref, b_ref, o_ref):
    k = pl.program_id(1)
    c = c_ref[...]
    s = (c * jax.nn.sigmoid(c)).astype(BF16)
    part = jnp.dot(s, w_ref[...].astype(BF16), preferred_element_type=F32)

    @pl.when(k == 0)
    def _():
        o_ref[...] = part + b_ref[...]

    @pl.when(k > 0)
    def _():
        o_ref[...] += part


def _ffn_kernel(x_ref, shift_ref, scale_ref, gate_ref, gpre_ref, gpost_ref,
                w1_ref, w3_ref, w2_ref, o_ref, act_scr):
    tl = x_ref.shape[0]
    sub_tl = min(tl, FFN_SUB_GROUPS)
    sub_rows = sub_tl * SUBLANES
    pre_gain = gpre_ref[...] * (1.0 + scale_ref[...])
    post_gain = 0.5 * gate_ref[...] * gpost_ref[...]
    shift = shift_ref[...]
    for s in range(tl // sub_tl):
        grp = slice(s * sub_tl, (s + 1) * sub_tl)
        rws = slice(s * sub_rows, (s + 1) * sub_rows)
        x = x_ref[grp]
        ms = jnp.mean(x * x, axis=-1, keepdims=True)
        h = x * lax.rsqrt(ms + EPS) * pre_gain + shift
        hb = h.reshape(sub_rows, D_MODEL).astype(BF16)
        for c in range(D_FF // FF_CHUNK):
            cols = slice(c * FF_CHUNK, (c + 1) * FF_CHUNK)
            g = jnp.dot(hb, w1_ref[:, cols], preferred_element_type=F32)
            v = jnp.dot(hb, w3_ref[:, cols], preferred_element_type=F32)
            act_scr[rws, cols] = (g * jax.nn.sigmoid(g) * v).astype(BF16)
        f = jnp.dot(act_scr[rws, :], w2_ref[...], preferred_element_type=F32)
        f = f.reshape(sub_tl, SUBLANES, D_MODEL)
        ms = jnp.mean(f * f, axis=-1, keepdims=True)
        o_ref[grp] = x_ref[grp] + f * lax.rsqrt(ms + EPS) * post_gain


def _build_halo(e_ref, prev_ref, taps, seg_len, chained):
    n = (taps - 1) * SUBLANES
    prev = prev_ref[...]
    if chained:
        cur = e_ref[seg_len * SUBLANES:seg_len * SUBLANES + n, :]
        sub = lax.broadcasted_iota(jnp.int32, cur.shape, 0) & (SUBLANES - 1)
        halo = jnp.where(sub == 0, pltpu.roll(prev, n - (SUBLANES - 1), axis=0),
                         pltpu.roll(cur, 1, axis=0))
    else:
        halo = prev
    e_ref[0:n, :] = halo


def _short_conv(e_ref, w_ref, taps, rows):
    acc = e_ref[0:rows, :] * w_ref[0:1, :]
    for k in range(1, taps):
        acc = acc + e_ref[k * SUBLANES:k * SUBLANES + rows, :] * w_ref[k:k + 1, :]
    return acc


def _mixer_kernel(x_ref, shift_ref, scale_ref, gate_ref, gpre_ref, gpost_ref,
                  win_ref, wout_ref, wca_ref, wcb_ref, bcb_ref,
                  wr_ref, br_ref, wi_ref, bi_ref, lam_ref,
                  wcc_ref, bcc_ref, lng_ref, lnb_ref, grp_ref,
                  sa_ref, sb_ref, sh_ref, sc_ref,
                  o_ref, na_ref, nb_ref, nh_ref, nc_ref,
                  u_scr, ea, eb, ec, a_scr, b_scr, hl_scr, p_scr, c_scr, y_scr,
                  *, chained):
    seg_len = x_ref.shape[0]
    rows = seg_len * SUBLANES

    @pl.when(pl.program_id(0) == 0)
    def _():
        na_ref[...] = sa_ref[...]
        nb_ref[...] = sb_ref[...]
        nh_ref[...] = sh_ref[...]
        nc_ref[...] = sc_ref[...]

    hb = _prenorm(x_ref, shift_ref, scale_ref, gpre_ref)
    for c in range(D_IN // MXU_DIM):
        cols = slice(c * MXU_DIM, (c + 1) * MXU_DIM)
        u_scr[:, cols] = jnp.dot(hb, win_ref[:, cols], preferred_element_type=F32)

    na = (CONV_A - 1) * SUBLANES
    ea[na:, :] = u_scr[:, O_AC:O_AC + D_A] * u_scr[:, O_AX:O_AX + D_A]
    _build_halo(ea, na_ref, CONV_A, seg_len, chained)
    ya = u_scr[:, O_AB:O_AB + D_A] * _short_conv(ea, wca_ref, CONV_A, rows)
    na_ref[...] = ea[rows:rows + na, :]
    y_scr[:, 0:D_A] = _rms(ya, grp_ref[:, 0:D_A]).astype(BF16)

    nb = (CONV_B - 1) * SUBLANES
    eb[nb:, :] = u_scr[:, O_BX:O_BX + D_B]
    _build_halo(eb, nb_ref, CONV_B, seg_len, chained)
    xb = _short_conv(eb, wcb_ref, CONV_B, rows) + bcb_ref[...]
    nb_ref[...] = eb[rows:rows + nb, :]
    xbb = xb.astype(BF16)
    half = D_B // 2
    gr = jnp.concatenate(
        [jnp.dot(xbb[:, :half], wr_ref[0], preferred_element_type=F32),
         jnp.dot(xbb[:, half:], wr_ref[1], preferred_element_type=F32)], axis=-1)
    gi = jnp.concatenate(
        [jnp.dot(xbb[:, :half], wi_ref[0], preferred_element_type=F32),
         jnp.dot(xbb[:, half:], wi_ref[1], preferred_element_type=F32)], axis=-1)
    r = jax.nn.sigmoid(gr + br_ref[...])
    ig = jax.nn.sigmoid(gi + bi_ref[...])
    log_a = RG_C * r * jax.nn.log_sigmoid(lam_ref[...])
    a = jnp.exp(log_a)
    a_scr[...] = a
    b_scr[...] = jnp.sqrt(-jnp.tanh(log_a) * (a * a + 1.0)) * ig * xb

    def scan_body(j, carry):
        hl, p = carry
        r0 = pl.multiple_of(j * SUBLANES, SUBLANES)
        a = a_scr[pl.ds(r0, SUBLANES), :]
        hl = a * hl + b_scr[pl.ds(r0, SUBLANES), :]
        p = a * p
        hl_scr[pl.ds(r0, SUBLANES), :] = hl
        p_scr[pl.ds(r0, SUBLANES), :] = p
        return hl, p

    seg_b, seg_a = lax.fori_loop(
        0, seg_len, scan_body,
        (jnp.zeros((SUBLANES, D_B), F32), jnp.ones((SUBLANES, D_B), F32)), unroll=8)

    prev_end = nh_ref[...]
    if chained:
        sub = lax.broadcasted_iota(jnp.int32, (SUBLANES, D_B), 0)
        ca, cb = seg_a, seg_b
        for d in (1, 2, 4):
            a_sh = jnp.where(sub >= d, pltpu.roll(ca, d, axis=0), 1.0)
            b_sh = jnp.where(sub >= d, pltpu.roll(cb, d, axis=0), 0.0)
            cb = ca * b_sh + cb
            ca = ca * a_sh
        c0 = jnp.broadcast_to(prev_end[SUBLANES - 1:SUBLANES, :], (SUBLANES, D_B))
        seg_end = ca * c0 + cb
        seg_start = jnp.where(sub == 0, c0, pltpu.roll(seg_end, 1, axis=0))
    else:
        seg_start = prev_end
        seg_end = seg_a * seg_start + seg_b
    nh_ref[...] = seg_end

    hl3 = hl_scr[...].reshape(seg_len, SUBLANES, D_B)
    p3 = p_scr[...].reshape(seg_len, SUBLANES, D_B)
    hseq = (hl3 + p3 * seg_start).reshape(rows, D_B)
    yb = hseq * jax.nn.gelu(u_scr[:, O_BG:O_BG + D_B])
    y_scr[:, D_A:D_A + D_B] = _rms(yb, grp_ref[:, D_A:D_A + D_B]).astype(BF16)

    nc = (CONV_C - 1) * SUBLANES
    ec[nc:, :] = u_scr[:, O_CV:O_CV + D_C] * jax.nn.sigmoid(u_scr[:, O_CG:O_CG + D_C])
    _build_halo(ec, nc_ref, CONV_C, seg_len, chained)

    def conv_body(blk, carry):
        r0 = pl.multiple_of(blk * CONV_ROWS, CONV_ROWS)
        acc = ec[pl.ds(r0, CONV_ROWS), :] * wcc_ref[0:1, :]
        for k in range(1, CONV_C):
            acc = acc + ec[pl.ds(r0 + k * SUBLANES, CONV_ROWS), :] * wcc_ref[k:k + 1, :]
        c_scr[pl.ds(r0, CONV_ROWS), :] = acc
        return carry

    lax.fori_loop(0, rows // CONV_ROWS, conv_body, 0)
    nc_ref[...] = ec[rows:rows + nc, :]
    yc = c_scr[...] + bcc_ref[...]
    mu = jnp.mean(yc, axis=-1, keepdims=True)
    var = jnp.mean(jnp.square(yc - mu), axis=-1, keepdims=True)
    yc = (yc - mu) * lax.rsqrt(var + EPS) * lng_ref[...] + lnb_ref[...]
    yc = yc * jax.nn.sigmoid(yc)
    y_scr[:, D_A + D_B:] = _rms(yc, grp_ref[:, D_A + D_B:]).astype(BF16)

    m = jnp.dot(y_scr[...], wout_ref[...], preferred_element_type=F32)
    mn = _rms(m, gpost_ref[...]).reshape(seg_len, SUBLANES, D_MODEL)
    o_ref[...] = x_ref[...] + gate_ref[...] * mn


def _const_spec(shape, index):
    return pl.BlockSpec(shape, lambda i: index, pipeline_mode=pl.Buffered(1))


def _mod_specs(layer, sub):
    return [_const_spec((None, SUBLANES, D_MODEL), (layer, 0, 3 * sub + k)) for k in range(3)]


def _ffn_call(x, mod, gpre, gpost, w1, w3, w2, layer, sub, which, tl):
    groups = x.shape[0]
    rows = tl * SUBLANES
    x_spec = pl.BlockSpec((tl, SUBLANES, D_MODEL), lambda i: (i, 0, 0))
    return pl.pallas_call(
        _ffn_kernel,
        grid=(groups // tl,),
        in_specs=[x_spec] + _mod_specs(layer, sub) + [
            _const_spec((None, 1, D_MODEL), (layer * N_SUB + sub, 0, 0)),
            _const_spec((None, 1, D_MODEL), (layer * N_SUB + sub, 0, 0)),
            _const_spec((None, None, D_MODEL, D_FF), (layer, which, 0, 0)),
            _const_spec((None, None, D_MODEL, D_FF), (layer, which, 0, 0)),
            _const_spec((None, None, D_FF, D_MODEL), (layer, which, 0, 0)),
        ],
        out_specs=x_spec,
        out_shape=jax.ShapeDtypeStruct(x.shape, F32),
        scratch_shapes=[pltpu.VMEM((rows, D_FF), BF16)],
        compiler_params=pltpu.CompilerParams(
            dimension_semantics=("parallel",), vmem_limit_bytes=VMEM_LIMIT),
        name=f"ffn_l{layer}_s{sub}",
    )(x, mod, mod, mod, gpre, gpost, w1, w3, w2)


def _mixer_call(x, mod, gpre, gpost, p, states, layer, seg_len, chained):
    groups = x.shape[0]
    rows = seg_len * SUBLANES
    sub = 1
    x_spec = pl.BlockSpec((seg_len, SUBLANES, D_MODEL), lambda i: (i, 0, 0))

    def lspec(arr):
        shape = (None,) + arr.shape[1:]
        return _const_spec(shape, (layer,) + (0,) * (arr.ndim - 1))

    def sspec(arr):
        return pl.BlockSpec(arr.shape[1:], lambda i: (0, 0))

    weights = [p["w_in"], p["w_out"], p["w_conv_a"], p["w_conv_b"], p["b_conv_b"],
               p["w_gate_r"], p["b_gate_r"], p["w_gate_i"], p["b_gate_i"], p["rg_lambda"],
               p["w_conv_c"], p["b_conv_c"], p["ln_c_g"], p["ln_c_b"], p["grp_g"]]
    state_shapes = [jax.ShapeDtypeStruct(s.shape[1:], F32) for s in states]
    outs = pl.pallas_call(
        functools.partial(_mixer_kernel, chained=chained),
        grid=(groups // seg_len,),
        in_specs=[x_spec] + _mod_specs(layer, sub) + [
            _const_spec((None, 1, D_MODEL), (layer * N_SUB + sub, 0, 0)),
            _const_spec((None, 1, D_MODEL), (layer * N_SUB + sub, 0, 0)),
        ] + [lspec(w) for w in weights] + [lspec(s) for s in states],
        out_specs=[x_spec] + [sspec(s) for s in states],
        out_shape=[jax.ShapeDtypeStruct(x.shape, F32)] + state_shapes,
        scratch_shapes=[
            pltpu.VMEM((rows, D_IN), F32),
            pltpu.VMEM((rows + (CONV_A - 1) * SUBLANES, D_A), F32),
            pltpu.VMEM((rows + (CONV_B - 1) * SUBLANES, D_B), F32),
            pltpu.VMEM((rows + (CONV_C - 1) * SUBLANES, D_C), F32),
            pltpu.VMEM((rows, D_B), F32),
            pltpu.VMEM((rows, D_B), F32),
            pltpu.VMEM((rows, D_B), F32),
            pltpu.VMEM((rows, D_B), F32),
            pltpu.VMEM((rows, D_C), F32),
            pltpu.VMEM((rows, D_MIX), BF16),
        ],
        compiler_params=pltpu.CompilerParams(
            dimension_semantics=("arbitrary",), vmem_limit_bytes=VMEM_LIMIT),
        name=f"mixer_l{layer}",
    )(x, mod, mod, mod, gpre, gpost, *weights, *states)
    return outs[0], outs[1:]


def _block_diag_halves(w):
    heads_per_half = MXU_DIM // HEAD_DIM_B
    w = w.reshape(DEPTH, 2, heads_per_half, HEAD_DIM_B, HEAD_DIM_B)
    eye = jnp.eye(heads_per_half, dtype=w.dtype)
    bd = jnp.einsum("lhaij,ab->lhaibj", w, eye)
    return bd.reshape(DEPTH, 2, MXU_DIM, MXU_DIM).astype(BF16)


def _trunk(x, mod, params, states, seg_len, chained):
    new_states = []
    for l in range(DEPTH):
        x = _ffn_call(x, mod, params["norm_pre"], params["norm_post"], params["ffn_w1"],
                      params["ffn_w3"], params["ffn_w2"], l, 0, 0, min(FFN_GROUPS, x.shape[0]))
        x, ns = _mixer_call(x, mod, params["norm_pre"], params["norm_post"], params, states,
                            l, seg_len, chained)
        new_states.append(ns)
        x = _ffn_call(x, mod, params["norm_pre"], params["norm_post"], params["ffn_w1"],
                      params["ffn_w3"], params["ffn_w2"], l, 2, 1, min(FFN_GROUPS, x.shape[0]))
    return x, [jnp.stack([ns[k] for ns in new_states]) for k in range(4)]


def kernel(x_prompt, x_sample, state_conv_a, state_conv_b, state_rglru, state_conv_c, c_prompt, c_sample, w_ada, b_ada, norm_pre, norm_post, ffn_w1, ffn_w3, ffn_w2, w_in, w_out, w_conv_a, w_conv_b, b_conv_b, w_gate_r, b_gate_r, w_gate_i, b_gate_i, rg_lambda, w_conv_c, b_conv_c, ln_c_g, ln_c_b, grp_g):
    batch, seq, _ = x_prompt.shape
    dec_batch, dec_seq, _ = x_sample.shape
    assert batch == 1 and dec_batch == SUBLANES
    assert seq % (SUBLANES * SEG_LEN) == 0 and SEG_LEN >= CONV_C - 1

    n_mod = 3 * N_SUB * D_MODEL
    c_all = jnp.concatenate(
        [c_prompt, c_sample, jnp.zeros((2 * SUBLANES - batch - dec_batch, D_MODEL), F32)], axis=0)
    mod_all = pl.pallas_call(
        _ada_kernel,
        grid=(DEPTH, D_MODEL // ADA_K),
        in_specs=[pl.BlockSpec((2 * SUBLANES, ADA_K), lambda l, k: (0, k)),
                  pl.BlockSpec((None, ADA_K, n_mod), lambda l, k: (l, k, 0)),
                  pl.BlockSpec((None, 1, n_mod), lambda l, k: (l, 0, 0))],
        out_specs=pl.BlockSpec((None, 2 * SUBLANES, n_mod), lambda l, k: (l, 0, 0)),
        out_shape=jax.ShapeDtypeStruct((DEPTH, 2 * SUBLANES, n_mod), F32),
        compiler_params=pltpu.CompilerParams(
            dimension_semantics=("parallel", "arbitrary"), vmem_limit_bytes=VMEM_LIMIT),
        name="adaln",
    )(c_all, w_ada, b_ada.reshape(DEPTH, 1, n_mod))
    mod_p = jnp.broadcast_to(mod_all[:, 0:1], (DEPTH, SUBLANES, n_mod))
    mod_s = mod_all[:, 1:1 + SUBLANES]

    def row(v):
        return v.reshape(DEPTH, 1, v.shape[-1])

    params = {
        "norm_pre": norm_pre.reshape(DEPTH * N_SUB, 1, D_MODEL),
        "norm_post": norm_post.reshape(DEPTH * N_SUB, 1, D_MODEL),
        "ffn_w1": ffn_w1.astype(BF16), "ffn_w3": ffn_w3.astype(BF16), "ffn_w2": ffn_w2.astype(BF16),
        "w_in": w_in.astype(BF16), "w_out": w_out.astype(BF16),
        "w_conv_a": w_conv_a, "w_conv_b": w_conv_b, "b_conv_b": row(b_conv_b),
        "w_gate_r": _block_diag_halves(w_gate_r), "b_gate_r": row(b_gate_r),
        "w_gate_i": _block_diag_halves(w_gate_i), "b_gate_i": row(b_gate_i),
        "rg_lambda": row(rg_lambda),
        "w_conv_c": w_conv_c, "b_conv_c": row(b_conv_c),
        "ln_c_g": row(ln_c_g), "ln_c_b": row(ln_c_b), "grp_g": row(grp_g),
    }

    tiles = seq // (SUBLANES * SEG_LEN)
    xp = x_prompt.reshape(tiles, SUBLANES, SEG_LEN, D_MODEL).transpose(0, 2, 1, 3)
    xp = xp.reshape(tiles * SEG_LEN, SUBLANES, D_MODEL)
    zero_states = [jnp.zeros((DEPTH, (CONV_A - 1) * SUBLANES, D_A), F32),
                   jnp.zeros((DEPTH, (CONV_B - 1) * SUBLANES, D_B), F32),
                   jnp.zeros((DEPTH, SUBLANES, D_B), F32),
                   jnp.zeros((DEPTH, (CONV_C - 1) * SUBLANES, D_C), F32)]
    yp, (pa, pb, ph, pc) = _trunk(xp, mod_p, params, zero_states, SEG_LEN, True)
    y_prompt = yp.reshape(tiles, SEG_LEN, SUBLANES, D_MODEL).transpose(0, 2, 1, 3)
    y_prompt = y_prompt.reshape(batch, seq, D_MODEL)

    def last_segment(s, taps_m1):
        return s.reshape(DEPTH, taps_m1, SUBLANES, s.shape[-1])[:, :, SUBLANES - 1][:, None]

    pa = last_segment(pa, CONV_A - 1)
    pb = last_segment(pb, CONV_B - 1)
    pc = last_segment(pc, CONV_C - 1)
    ph = ph[:, SUBLANES - 1:SUBLANES]

    def to_rows(s):
        return s.transpose(0, 2, 1, 3).reshape(DEPTH, s.shape[2] * SUBLANES, s.shape[3])

    def from_rows(s, taps_m1):
        return s.reshape(DEPTH, taps_m1, SUBLANES, s.shape[-1]).transpose(0, 2, 1, 3)

    xs = x_sample.transpose(1, 0, 2)
    s_states = [to_rows(state_conv_a), to_rows(state_conv_b), state_rglru, to_rows(state_conv_c)]
    ys, (sa, sb, sh, sc) = _trunk(xs, mod_s, params, s_states, dec_seq, False)
    y_sample = ys.transpose(1, 0, 2)
    sa = from_rows(sa, CONV_A - 1)
    sb = from_rows(sb, CONV_B - 1)
    sc = from_rows(sc, CONV_C - 1)

    return (y_prompt, y_sample, pa, pb, ph, pc, sa, sb, sh, sc)
```

```python
import functools

import jax
import jax.numpy as jnp
from jax import lax
from jax.experimental import pallas as pl
from jax.experimental.pallas import tpu as pltpu

D_MODEL = 1024
DEPTH = 4
D_A = 256
D_B = 512
HEADS_B = 8
HEAD_DIM_B = D_B // HEADS_B
D_C = 256
D_MIX = D_A + D_B + D_C
D_IN = 3 * D_A + 2 * D_B + 2 * D_C
CONV_A = 3
CONV_B = 4
CONV_C = 31
RG_C = 8.0
D_FF = 2816
N_SUB = 3
EPS = 1e-6

SUBLANES = 8
MXU_DIM = 256
SEG_LEN = 64
FF_CHUNK = MXU_DIM
CONV_ROWS = 64
ADA_K = MXU_DIM
FFN_GROUPS = 128
FFN_SUB_GROUPS = 32
VMEM_LIMIT = 56 * 1024 * 1024

O_AB, O_AC, O_AX = 0, D_A, 2 * D_A
O_BX = 3 * D_A
O_BG = O_BX + D_B
O_CV = O_BG + D_B
O_CG = O_CV + D_C

F32 = jnp.float32
BF16 = jnp.bfloat16


def _rms(x, g):
    ms = jnp.mean(x * x, axis=-1, keepdims=True)
    return x * lax.rsqrt(ms + EPS) * g


def _ada_kernel(c_ref, w_ref, b_ref, o_ref):
    k = pl.program_id(1)
    c = c_ref[...]
    s = (c * jax.nn.sigmoid(c)).astype(BF16)
    part = jnp.dot(s, w_ref[...].astype(BF16), preferred_element_type=F32)

    @pl.when(k == 0)
    def _():
        o_ref[...] = part + b_ref[...]

    @pl.when(k > 0)
    def _():
        o_ref[...] += part


def _cast_next_weights(src_refs, dst_refs):
    for src, dst in zip(src_refs, dst_refs, strict=True):
        dst[...] = src[...].astype(BF16)


def _ffn_kernel(*refs, n_next):
    (x_ref, shift_ref, scale_ref, gate_ref, gpre_ref, gpost_ref,
     w1_ref, w3_ref, w2_ref) = refs[:9]
    next_src = refs[9:9 + n_next]
    o_ref = refs[9 + n_next]
    next_dst = refs[10 + n_next:10 + 2 * n_next]
    act_scr = refs[10 + 2 * n_next]
    _cast_next_weights(next_src, next_dst)
    tl = x_ref.shape[0]
    sub_tl = min(tl, FFN_SUB_GROUPS)
    sub_rows = sub_tl * SUBLANES
    pre_gain = gpre_ref[...] * (1.0 + scale_ref[...])
    post_gain = 0.5 * gate_ref[...] * gpost_ref[...]
    shift = shift_ref[...]
    for s in range(tl // sub_tl):
        grp = slice(s * sub_tl, (s + 1) * sub_tl)
        rws = slice(s * sub_rows, (s + 1) * sub_rows)
        x = x_ref[grp]
        ms = jnp.mean(x * x, axis=-1, keepdims=True)
        h = x * lax.rsqrt(ms + EPS) * pre_gain + shift
        hb = h.reshape(sub_rows, D_MODEL).astype(BF16)
        for c in range(D_FF // FF_CHUNK):
            cols = slice(c * FF_CHUNK, (c + 1) * FF_CHUNK)
            g = jnp.dot(hb, w1_ref[:, cols], preferred_element_type=F32)
            v = jnp.dot(hb, w3_ref[:, cols], preferred_element_type=F32)
            act_scr[rws, cols] = (g * jax.nn.sigmoid(g) * v).astype(BF16)
        f = jnp.dot(act_scr[rws, :], w2_ref[...], preferred_element_type=F32)
        f = f.reshape(sub_tl, SUBLANES, D_MODEL)
        ms = jnp.mean(f * f, axis=-1, keepdims=True)
        o_ref[grp] = x_ref[grp] + f * lax.rsqrt(ms + EPS) * post_gain


def _build_halo(e_ref, prev_ref, taps, seg_len, chained):
    n = (taps - 1) * SUBLANES
    prev = prev_ref[...]
    if chained:
        cur = e_ref[seg_len * SUBLANES:seg_len * SUBLANES + n, :]
        sub = lax.broadcasted_iota(jnp.int32, cur.shape, 0) & (SUBLANES - 1)
        halo = jnp.where(sub == 0, pltpu.roll(prev, n - (SUBLANES - 1), axis=0),
                         pltpu.roll(cur, 1, axis=0))
    else:
        halo = prev
    e_ref[0:n, :] = halo


def _short_conv(e_ref, w_ref, taps, rows):
    acc = e_ref[0:rows, :] * w_ref[0:1, :]
    for k in range(1, taps):
        acc = acc + e_ref[k * SUBLANES:k * SUBLANES + rows, :] * w_ref[k:k + 1, :]
    return acc


def _stage_in(x, u_dst, pre_gain, shift, win_ref, hb_scr):
    rows = x.shape[0] * SUBLANES
    ms = jnp.mean(x * x, axis=-1, keepdims=True)
    h = x * lax.rsqrt(ms + EPS) * pre_gain + shift
    hb_scr[...] = h.reshape(rows, D_MODEL).astype(BF16)
    for c in range(D_IN // MXU_DIM):
        cols = slice(c * MXU_DIM, (c + 1) * MXU_DIM)
        u_dst[:, cols] = jnp.dot(hb_scr[...], win_ref[:, cols], preferred_element_type=F32)


def _stage_mix(u, x_view, o_view, post_gain, wout_ref, wca_ref, wcb_ref, bcb_ref,
               wr_ref, br_ref, wi_ref, bi_ref, lam_ref, wcc_ref, bcc_ref, lng_ref, lnb_ref,
               grp_ref, na_ref, nb_ref, nh_ref, nc_ref, ea, eb, ec, a_scr, b_scr, hl_scr, p_scr,
               y_scr, chained):
    seg_len = x_view.shape[0]
    rows = seg_len * SUBLANES

    na = (CONV_A - 1) * SUBLANES
    ea[na:, :] = u[:, O_AC:O_AC + D_A] * u[:, O_AX:O_AX + D_A]
    _build_halo(ea, na_ref, CONV_A, seg_len, chained)
    ya = u[:, O_AB:O_AB + D_A] * _short_conv(ea, wca_ref, CONV_A, rows)
    na_ref[...] = ea[rows:rows + na, :]
    y_scr[:, 0:D_A] = _rms(ya, grp_ref[:, 0:D_A]).astype(BF16)

    nb = (CONV_B - 1) * SUBLANES
    eb[nb:, :] = u[:, O_BX:O_BX + D_B]
    _build_halo(eb, nb_ref, CONV_B, seg_len, chained)
    xb = _short_conv(eb, wcb_ref, CONV_B, rows) + bcb_ref[...]
    nb_ref[...] = eb[rows:rows + nb, :]
    xbb = xb.astype(BF16)
    half = D_B // 2
    gr = jnp.concatenate(
        [jnp.dot(xbb[:, :half], wr_ref[0], preferred_element_type=F32),
         jnp.dot(xbb[:, half:], wr_ref[1], preferred_element_type=F32)], axis=-1)
    gi = jnp.concatenate(
        [jnp.dot(xbb[:, :half], wi_ref[0], preferred_element_type=F32),
         jnp.dot(xbb[:, half:], wi_ref[1], preferred_element_type=F32)], axis=-1)
    r = jax.nn.sigmoid(gr + br_ref[...])
    ig = jax.nn.sigmoid(gi + bi_ref[...])
    log_a = RG_C * r * jax.nn.log_sigmoid(lam_ref[...])
    a = jnp.exp(log_a)
    a_scr[...] = a
    b_scr[...] = jnp.sqrt(-jnp.tanh(log_a) * (a * a + 1.0)) * ig * xb

    hl = jnp.zeros((SUBLANES, D_B), F32)
    p = jnp.ones((SUBLANES, D_B), F32)
    for j in range(seg_len):
        grp_rows = slice(j * SUBLANES, (j + 1) * SUBLANES)
        aj = a_scr[grp_rows, :]
        hl = aj * hl + b_scr[grp_rows, :]
        p = aj * p
        hl_scr[grp_rows, :] = hl
        p_scr[grp_rows, :] = p

    prev_end = nh_ref[...]
    if chained:
        sub = lax.broadcasted_iota(jnp.int32, (SUBLANES, D_B), 0)
        ca, cb = p, hl
        for d in (1, 2, 4):
            a_sh = jnp.where(sub >= d, pltpu.roll(ca, d, axis=0), 1.0)
            b_sh = jnp.where(sub >= d, pltpu.roll(cb, d, axis=0), 0.0)
            cb = ca * b_sh + cb
            ca = ca * a_sh
        c0 = jnp.broadcast_to(prev_end[SUBLANES - 1:SUBLANES, :], (SUBLANES, D_B))
        seg_end = ca * c0 + cb
        seg_start = jnp.where(sub == 0, c0, pltpu.roll(seg_end, 1, axis=0))
    else:
        seg_start = prev_end
        seg_end = p * seg_start + hl
    nh_ref[...] = seg_end

    hl3 = hl_scr[...].reshape(seg_len, SUBLANES, D_B)
    p3 = p_scr[...].reshape(seg_len, SUBLANES, D_B)
    hseq = (hl3 + p3 * seg_start).reshape(rows, D_B)
    yb = hseq * jax.nn.gelu(u[:, O_BG:O_BG + D_B])
    y_scr[:, D_A:D_A + D_B] = _rms(yb, grp_ref[:, D_A:D_A + D_B]).astype(BF16)

    nc = (CONV_C - 1) * SUBLANES
    ec[nc:, :] = u[:, O_CV:O_CV + D_C] * jax.nn.sigmoid(u[:, O_CG:O_CG + D_C])
    _build_halo(ec, nc_ref, CONV_C, seg_len, chained)
    for blk in range(rows // CONV_ROWS):
        r0 = blk * CONV_ROWS
        acc = ec[r0:r0 + CONV_ROWS, :] * wcc_ref[0:1, :]
        for k in range(1, CONV_C):
            acc = acc + ec[r0 + k * SUBLANES:r0 + k * SUBLANES + CONV_ROWS, :] * wcc_ref[k:k + 1, :]
        yc = acc + bcc_ref[...]
        mu = jnp.mean(yc, axis=-1, keepdims=True)
        var = jnp.mean(jnp.square(yc - mu), axis=-1, keepdims=True)
        yc = (yc - mu) * lax.rsqrt(var + EPS) * lng_ref[...] + lnb_ref[...]
        yc = yc * jax.nn.sigmoid(yc)
        y_scr[r0:r0 + CONV_ROWS, D_A + D_B:] = _rms(yc, grp_ref[:, D_A + D_B:]).astype(BF16)
    nc_ref[...] = ec[rows:rows + nc, :]

    m = jnp.dot(y_scr[...], wout_ref[...], preferred_element_type=F32)
    m = m.reshape(seg_len, SUBLANES, D_MODEL)
    ms = jnp.mean(m * m, axis=-1, keepdims=True)
    o_view[...] = x_view[...] + m * lax.rsqrt(ms + EPS) * post_gain


def _mixer_kernel(*refs, chained, pipelined, n_next):
    if pipelined:
        x_ref, xn_ref = refs[:2]
        refs = refs[2:]
    else:
        x_ref, xn_ref = refs[0], None
        refs = refs[1:]
    (shift_ref, scale_ref, gate_ref, gpre_ref, gpost_ref, win_ref, wout_ref,
     wca_ref, wcb_ref, bcb_ref, wr_ref, br_ref, wi_ref, bi_ref, lam_ref,
     wcc_ref, bcc_ref, lng_ref, lnb_ref, grp_ref,
     sa_ref, sb_ref, sh_ref, sc_ref) = refs[:24]
    next_src = refs[24:24 + n_next]
    o_ref, na_ref, nb_ref, nh_ref, nc_ref = refs[24 + n_next:29 + n_next]
    next_dst = refs[29 + n_next:29 + 2 * n_next]
    (u_scr, hb_scr, ea, eb, ec, a_scr, b_scr, hl_scr, p_scr, y_scr) = refs[29 + 2 * n_next:]
    _cast_next_weights(next_src, next_dst)
    seg_len = x_ref.shape[0] // (2 if pipelined else 1)
    pre_gain = gpre_ref[...] * (1.0 + scale_ref[...])
    post_gain = gate_ref[...] * gpost_ref[...]
    shift = shift_ref[...]
    stage_in = functools.partial(_stage_in, pre_gain=pre_gain, shift=shift, win_ref=win_ref,
                                 hb_scr=hb_scr)
    stage_mix = functools.partial(
        _stage_mix, post_gain=post_gain, wout_ref=wout_ref, wca_ref=wca_ref, wcb_ref=wcb_ref,
        bcb_ref=bcb_ref, wr_ref=wr_ref, br_ref=br_ref, wi_ref=wi_ref, bi_ref=bi_ref,
        lam_ref=lam_ref, wcc_ref=wcc_ref, bcc_ref=bcc_ref, lng_ref=lng_ref, lnb_ref=lnb_ref,
        grp_ref=grp_ref, na_ref=na_ref, nb_ref=nb_ref, nh_ref=nh_ref, nc_ref=nc_ref,
        ea=ea, eb=eb, ec=ec, a_scr=a_scr, b_scr=b_scr, hl_scr=hl_scr, p_scr=p_scr, y_scr=y_scr,
        chained=chained)

    @pl.when(pl.program_id(0) == 0)
    def _():
        na_ref[...] = sa_ref[...]
        nb_ref[...] = sb_ref[...]
        nh_ref[...] = sh_ref[...]
        nc_ref[...] = sc_ref[...]
        if pipelined:
            stage_in(x_ref[0:seg_len], u_scr.at[0])

    if pipelined:
        first, second = pl.ds(0, seg_len), pl.ds(seg_len, seg_len)
        stage_in(x_ref[seg_len:2 * seg_len], u_scr.at[1])
        stage_mix(u_scr.at[0], x_ref.at[first], o_ref.at[first])
        stage_in(xn_ref[...], u_scr.at[0])
        stage_mix(u_scr.at[1], x_ref.at[second], o_ref.at[second])
    else:
        stage_in(x_ref[...], u_scr.at[0])
        stage_mix(u_scr.at[0], x_ref, o_ref)


def _const_spec(shape, index):
    return pl.BlockSpec(shape, lambda i: index, pipeline_mode=pl.Buffered(1))


def _mod_specs(layer, sub):
    return [_const_spec((None, SUBLANES, D_MODEL), (layer, 0, 3 * sub + k)) for k in range(3)]


def _next_weight_specs(next_w, steps):
    in_specs, out_specs, out_shapes = [], [], []
    for arr, lead in next_w:
        r, c = arr.shape[-2:]
        assert r % (steps * 2 * SUBLANES) == 0
        blk = r // steps
        in_specs.append(pl.BlockSpec((None,) * len(lead) + (blk, c),
                                     lambda i, lead=lead: lead + (i, 0)))
        out_specs.append(pl.BlockSpec((blk, c), lambda i: (i, 0)))
        out_shapes.append(jax.ShapeDtypeStruct((r, c), BF16))
    return in_specs, out_specs, out_shapes


def _ffn_call(x, mod, gpre, gpost, w_bf16, next_w, layer, sub, tl):
    groups = x.shape[0]
    rows = tl * SUBLANES
    steps = groups // tl
    x_spec = pl.BlockSpec((tl, SUBLANES, D_MODEL), lambda i: (i, 0, 0))
    nin, nout, nshape = _next_weight_specs(next_w, steps)
    outs = pl.pallas_call(
        functools.partial(_ffn_kernel, n_next=len(next_w)),
        grid=(steps,),
        in_specs=[x_spec] + _mod_specs(layer, sub) + [
            _const_spec((None, 1, D_MODEL), (layer * N_SUB + sub, 0, 0)),
            _const_spec((None, 1, D_MODEL), (layer * N_SUB + sub, 0, 0)),
            _const_spec((D_MODEL, D_FF), (0, 0)),
            _const_spec((D_MODEL, D_FF), (0, 0)),
            _const_spec((D_FF, D_MODEL), (0, 0)),
        ] + nin,
        out_specs=[x_spec] + nout,
        out_shape=[jax.ShapeDtypeStruct(x.shape, F32)] + nshape,
        scratch_shapes=[pltpu.VMEM((rows, D_FF), BF16)],
        compiler_params=pltpu.CompilerParams(
            dimension_semantics=("parallel",), vmem_limit_bytes=VMEM_LIMIT),
        name=f"ffn_l{layer}_s{sub}",
    )(x, mod, mod, mod, gpre, gpost, *w_bf16, *[arr for arr, _ in next_w])
    return outs[0], tuple(outs[1:])


def _mixer_call(x, mod, gpre, gpost, w_bf16, next_w, p, states, layer, seg_len, chained):
    groups = x.shape[0]
    n_tiles = groups // seg_len
    pipelined = n_tiles > 1
    tiles_per_step = 2 if pipelined else 1
    assert n_tiles % tiles_per_step == 0
    rows = seg_len * SUBLANES
    sub = 1
    x_spec = pl.BlockSpec((tiles_per_step * seg_len, SUBLANES, D_MODEL), lambda i: (i, 0, 0))
    next_spec = pl.BlockSpec((seg_len, SUBLANES, D_MODEL),
                             lambda i: (jnp.minimum(2 * i + 2, n_tiles - 1), 0, 0))

    def lspec(arr):
        shape = (None,) + arr.shape[1:]
        return _const_spec(shape, (layer,) + (0,) * (arr.ndim - 1))

    def sspec(arr):
        return pl.BlockSpec(arr.shape[1:], lambda i: (0, 0))

    w_in, w_out = w_bf16
    weights = [p["w_conv_a"], p["w_conv_b"], p["b_conv_b"],
               p["w_gate_r"], p["b_gate_r"], p["w_gate_i"], p["b_gate_i"], p["rg_lambda"],
               p["w_conv_c"], p["b_conv_c"], p["ln_c_g"], p["ln_c_b"], p["grp_g"]]
    state_shapes = [jax.ShapeDtypeStruct(s.shape[1:], F32) for s in states]
    x_args, x_specs = ([x, x], [x_spec, next_spec]) if pipelined else ([x], [x_spec])
    steps = n_tiles // tiles_per_step
    nin, nout, nshape = _next_weight_specs(next_w, steps)
    outs = pl.pallas_call(
        functools.partial(_mixer_kernel, chained=chained, pipelined=pipelined,
                          n_next=len(next_w)),
        grid=(steps,),
        in_specs=x_specs + _mod_specs(layer, sub) + [
            _const_spec((None, 1, D_MODEL), (layer * N_SUB + sub, 0, 0)),
            _const_spec((None, 1, D_MODEL), (layer * N_SUB + sub, 0, 0)),
            _const_spec(w_in.shape, (0, 0)),
            _const_spec(w_out.shape, (0, 0)),
        ] + [lspec(w) for w in weights] + [lspec(s) for s in states] + nin,
        out_specs=[x_spec] + [sspec(s) for s in states] + nout,
        out_shape=[jax.ShapeDtypeStruct(x.shape, F32)] + state_shapes + nshape,
        scratch_shapes=[
            pltpu.VMEM((tiles_per_step, rows, D_IN), F32),
            pltpu.VMEM((rows, D_MODEL), BF16),
            pltpu.VMEM((rows + (CONV_A - 1) * SUBLANES, D_A), F32),
            pltpu.VMEM((rows + (CONV_B - 1) * SUBLANES, D_B), F32),
            pltpu.VMEM((rows + (CONV_C - 1) * SUBLANES, D_C), F32),
            pltpu.VMEM((rows, D_B), F32),
            pltpu.VMEM((rows, D_B), F32),
            pltpu.VMEM((rows, D_B), F32),
            pltpu.VMEM((rows, D_B), F32),
            pltpu.VMEM((rows, D_MIX), BF16),
        ],
        compiler_params=pltpu.CompilerParams(
            dimension_semantics=("arbitrary",), vmem_limit_bytes=VMEM_LIMIT),
        name=f"mixer_l{layer}",
    )(*x_args, mod, mod, mod, gpre, gpost, w_in, w_out, *weights, *states,
      *[arr for arr, _ in next_w])
    return outs[0], outs[1:5], tuple(outs[5:])


def _block_diag_halves(w):
    heads_per_half = MXU_DIM // HEAD_DIM_B
    w = w.reshape(DEPTH, 2, heads_per_half, HEAD_DIM_B, HEAD_DIM_B)
    eye = jnp.eye(heads_per_half, dtype=w.dtype)
    bd = jnp.einsum("lhaij,ab->lhaibj", w, eye)
    return bd.reshape(DEPTH, 2, MXU_DIM, MXU_DIM).astype(BF16)


def _trunk(x, mod, params, states, seg_len, chained, w_bf16, w_f32):
    def f32_weights(key):
        if key[0] == "ffn":
            lead = key[1:]
            return [(w_f32["ffn_w1"], lead), (w_f32["ffn_w3"], lead), (w_f32["ffn_w2"], lead)]
        return [(w_f32["w_in"], key[1:]), (w_f32["w_out"], key[1:])]

    order = []
    for l in range(DEPTH):
        order += [("ffn", l, 0), ("mix", l), ("ffn", l, 1)]
    new_states = []
    gpre, gpost = params["norm_pre"], params["norm_post"]
    for n, key in enumerate(order):
        nxt = order[n + 1] if n + 1 < len(order) else None
        next_w = f32_weights(nxt) if (w_f32 is not None and nxt is not None) else []
        l = key[1]
        if key[0] == "ffn":
            x, cast = _ffn_call(x, mod, gpre, gpost, w_bf16[key], next_w, l, 2 * key[2],
                                min(FFN_GROUPS, x.shape[0]))
        else:
            x, ns, cast = _mixer_call(x, mod, gpre, gpost, w_bf16[key], next_w, params, states,
                                      l, seg_len, chained)
            new_states.append(ns)
        if next_w:
            w_bf16[nxt] = cast
    return x, [jnp.stack([ns[k] for ns in new_states]) for k in range(4)]


def kernel(x_prompt, x_sample, state_conv_a, state_conv_b, state_rglru, state_conv_c, c_prompt, c_sample, w_ada, b_ada, norm_pre, norm_post, ffn_w1, ffn_w3, ffn_w2, w_in, w_out, w_conv_a, w_conv_b, b_conv_b, w_gate_r, b_gate_r, w_gate_i, b_gate_i, rg_lambda, w_conv_c, b_conv_c, ln_c_g, ln_c_b, grp_g):
    batch, seq, _ = x_prompt.shape
    dec_batch, dec_seq, _ = x_sample.shape
    assert batch == 1 and dec_batch == SUBLANES
    assert seq % (SUBLANES * SEG_LEN) == 0 and SEG_LEN >= CONV_C - 1

    n_mod = 3 * N_SUB * D_MODEL
    c_all = jnp.concatenate(
        [c_prompt, c_sample, jnp.zeros((2 * SUBLANES - batch - dec_batch, D_MODEL), F32)], axis=0)
    mod_all = pl.pallas_call(
        _ada_kernel,
        grid=(DEPTH, D_MODEL // ADA_K),
        in_specs=[pl.BlockSpec((2 * SUBLANES, ADA_K), lambda l, k: (0, k)),
                  pl.BlockSpec((None, ADA_K, n_mod), lambda l, k: (l, k, 0)),
                  pl.BlockSpec((None, 1, n_mod), lambda l, k: (l, 0, 0))],
        out_specs=pl.BlockSpec((None, 2 * SUBLANES, n_mod), lambda l, k: (l, 0, 0)),
        out_shape=jax.ShapeDtypeStruct((DEPTH, 2 * SUBLANES, n_mod), F32),
        compiler_params=pltpu.CompilerParams(
            dimension_semantics=("parallel", "arbitrary"), vmem_limit_bytes=VMEM_LIMIT),
        name="adaln",
    )(c_all, w_ada, b_ada.reshape(DEPTH, 1, n_mod))
    mod_p = jnp.broadcast_to(mod_all[:, 0:1], (DEPTH, SUBLANES, n_mod))
    mod_s = mod_all[:, 1:1 + SUBLANES]

    def row(v):
        return v.reshape(DEPTH, 1, v.shape[-1])

    params = {
        "norm_pre": norm_pre.reshape(DEPTH * N_SUB, 1, D_MODEL),
        "norm_post": norm_post.reshape(DEPTH * N_SUB, 1, D_MODEL),
        "w_conv_a": w_conv_a, "w_conv_b": w_conv_b, "b_conv_b": row(b_conv_b),
        "w_gate_r": _block_diag_halves(w_gate_r), "b_gate_r": row(b_gate_r),
        "w_gate_i": _block_diag_halves(w_gate_i), "b_gate_i": row(b_gate_i),
        "rg_lambda": row(rg_lambda),
        "w_conv_c": w_conv_c, "b_conv_c": row(b_conv_c),
        "ln_c_g": row(ln_c_g), "ln_c_b": row(ln_c_b), "grp_g": row(grp_g),
    }

    tiles = seq // (SUBLANES * SEG_LEN)
    xp = x_prompt.reshape(tiles, SUBLANES, SEG_LEN, D_MODEL).transpose(0, 2, 1, 3)
    xp = xp.reshape(tiles * SEG_LEN, SUBLANES, D_MODEL)
    zero_states = [jnp.zeros((DEPTH, (CONV_A - 1) * SUBLANES, D_A), F32),
                   jnp.zeros((DEPTH, (CONV_B - 1) * SUBLANES, D_B), F32),
                   jnp.zeros((DEPTH, SUBLANES, D_B), F32),
                   jnp.zeros((DEPTH, (CONV_C - 1) * SUBLANES, D_C), F32)]
    w_f32 = {"ffn_w1": ffn_w1, "ffn_w3": ffn_w3, "ffn_w2": ffn_w2, "w_in": w_in, "w_out": w_out}
    w_bf16 = {("ffn", 0, 0): (ffn_w1[0, 0].astype(BF16), ffn_w3[0, 0].astype(BF16),
                              ffn_w2[0, 0].astype(BF16))}
    yp, (pa, pb, ph, pc) = _trunk(xp, mod_p, params, zero_states, SEG_LEN, True, w_bf16, w_f32)
    y_prompt = yp.reshape(tiles, SEG_LEN, SUBLANES, D_MODEL).transpose(0, 2, 1, 3)
    y_prompt = y_prompt.reshape(batch, seq, D_MODEL)

    def last_segment(s, taps_m1):
        return s.reshape(DEPTH, taps_m1, SUBLANES, s.shape[-1])[:, :, SUBLANES - 1][:, None]

    pa = last_segment(pa, CONV_A - 1)
    pb = last_segment(pb, CONV_B - 1)
    pc = last_segment(pc, CONV_C - 1)
    ph = ph[:, SUBLANES - 1:SUBLANES]

    def to_rows(s):
        return s.transpose(0, 2, 1, 3).reshape(DEPTH, s.shape[2] * SUBLANES, s.shape[3])

    def from_rows(s, taps_m1):
        return s.reshape(DEPTH, taps_m1, SUBLANES, s.shape[-1]).transpose(0, 2, 1, 3)

    xs = x_sample.transpose(1, 0, 2)
    s_states = [to_rows(state_conv_a), to_rows(state_conv_b), state_rglru, to_rows(state_conv_c)]
    ys, (sa, sb, sh, sc) = _trunk(xs, mod_s, params, s_states, dec_seq, False, w_bf16, None)
    y_sample = ys.transpose(1, 0, 2)
    sa = from_rows(sa, CONV_A - 1)
    sb = from_rows(sb, CONV_B - 1)
    sc = from_rows(sc, CONV_C - 1)

    return (y_prompt, y_sample, pa, pb, ph, pc, sa, sb, sh, sc)
```

```python
import functools

import jax
import jax.numpy as jnp
from jax import lax
from jax.experimental import pallas as pl
from jax.experimental.pallas import tpu as pltpu

D_MODEL = 1024
DEPTH = 4
D_A = 256
D_B = 512
HEADS_B = 8
HEAD_DIM_B = D_B // HEADS_B
D_C = 256
D_MIX = D_A + D_B + D_C
D_IN = 3 * D_A + 2 * D_B + 2 * D_C
CONV_A = 3
CONV_B = 4
CONV_C = 31
RG_C = 8.0
D_FF = 2816
N_SUB = 3
EPS = 1e-6

SUBLANES = 8
MXU_DIM = 256
SEG_LEN = 64
FF_CHUNK = MXU_DIM
CONV_ROWS = 64
ADA_K = MXU_DIM
ADA_SPLIT = 4
FFN_GROUPS = 128
FFN_SUB_GROUPS = 32
VMEM_LIMIT = 56 * 1024 * 1024

O_AB, O_AC, O_AX = 0, D_A, 2 * D_A
O_BX = 3 * D_A
O_BG = O_BX + D_B
O_CV = O_BG + D_B
O_CG = O_CV + D_C

F32 = jnp.float32
BF16 = jnp.bfloat16


def _rms(x, g):
    ms = jnp.mean(x * x, axis=-1, keepdims=True)
    return x * lax.rsqrt(ms + EPS) * g


def _ada_kernel(c_ref, *refs):
    w_refs, b_ref, o_ref = refs[:ADA_SPLIT], refs[ADA_SPLIT], refs[ADA_SPLIT + 1]
    k = pl.program_id(1)
    c = c_ref[...]
    s = (c * jax.nn.sigmoid(c)).astype(BF16)
    w = jnp.concatenate([w_ref[...] for w_ref in w_refs], axis=0).astype(BF16)
    part = jnp.dot(s, w, preferred_element_type=F32)

    @pl.when(k == 0)
    def _():
        o_ref[...] = part + b_ref[...]

    @pl.when(k > 0)
    def _():
        o_ref[...] += part


def _cast_next_weights(src_refs, dst_refs):
    for src, dst in zip(src_refs, dst_refs, strict=True):
        dst[...] = src[...].astype(BF16)


def _ffn_rows(x, pre_gain, shift, post_gain, w1_ref, w3_ref, w2_ref, act):
    groups = x.shape[0]
    rows = groups * SUBLANES
    ms = jnp.mean(x * x, axis=-1, keepdims=True)
    h = x * lax.rsqrt(ms + EPS) * pre_gain + shift
    hb = h.reshape(rows, D_MODEL).astype(BF16)
    for c in range(D_FF // FF_CHUNK):
        cols = slice(c * FF_CHUNK, (c + 1) * FF_CHUNK)
        g = jnp.dot(hb, w1_ref[:, cols], preferred_element_type=F32)
        v = jnp.dot(hb, w3_ref[:, cols], preferred_element_type=F32)
        act[:, cols] = (g * jax.nn.sigmoid(g) * v).astype(BF16)
    f = jnp.dot(act[...], w2_ref[...], preferred_element_type=F32)
    f = f.reshape(groups, SUBLANES, D_MODEL)
    ms = jnp.mean(f * f, axis=-1, keepdims=True)
    return x + f * lax.rsqrt(ms + EPS) * post_gain


def _ffn_kernel(*refs, n_next):
    (x_ref, shift_ref, scale_ref, gate_ref, xs_ref, shift_s_ref, scale_s_ref, gate_s_ref,
     gpre_ref, gpost_ref, w1_ref, w3_ref, w2_ref) = refs[:13]
    next_src = refs[13:13 + n_next]
    o_ref, os_ref = refs[13 + n_next:15 + n_next]
    next_dst = refs[15 + n_next:15 + 2 * n_next]
    act_scr = refs[15 + 2 * n_next]
    _cast_next_weights(next_src, next_dst)
    rows_fn = functools.partial(_ffn_rows, w1_ref=w1_ref, w3_ref=w3_ref, w2_ref=w2_ref)

    @pl.when(pl.program_id(0) == 0)
    def _():
        rows_s = xs_ref.shape[0] * SUBLANES
        os_ref[...] = rows_fn(xs_ref[...], gpre_ref[...] * (1.0 + scale_s_ref[...]),
                              shift_s_ref[...], 0.5 * gate_s_ref[...] * gpost_ref[...],
                              act=act_scr.at[pl.ds(0, rows_s)])

    tl = x_ref.shape[0]
    sub_tl = min(tl, FFN_SUB_GROUPS)
    sub_rows = sub_tl * SUBLANES
    pre_gain = gpre_ref[...] * (1.0 + scale_ref[...])
    post_gain = 0.5 * gate_ref[...] * gpost_ref[...]
    shift = shift_ref[...]
    for s in range(tl // sub_tl):
        grp = slice(s * sub_tl, (s + 1) * sub_tl)
        o_ref[grp] = rows_fn(x_ref[grp], pre_gain, shift, post_gain,
                             act=act_scr.at[pl.ds(s * sub_rows, sub_rows)])


def _build_halo(e_ref, prev_ref, taps, seg_len, chained):
    n = (taps - 1) * SUBLANES
    prev = prev_ref[...]
    if chained:
        cur = e_ref[seg_len * SUBLANES:seg_len * SUBLANES + n, :]
        sub = lax.broadcasted_iota(jnp.int32, cur.shape, 0) & (SUBLANES - 1)
        halo = jnp.where(sub == 0, pltpu.roll(prev, n - (SUBLANES - 1), axis=0),
                         pltpu.roll(cur, 1, axis=0))
    else:
        halo = prev
    e_ref[0:n, :] = halo


def _short_conv(e_ref, w_ref, taps, rows):
    acc = e_ref[0:rows, :] * w_ref[0:1, :]
    for k in range(1, taps):
        acc = acc + e_ref[k * SUBLANES:k * SUBLANES + rows, :] * w_ref[k:k + 1, :]
    return acc


def _stage_in(x, u_dst, pre_gain, shift, win_ref):
    rows = x.shape[0] * SUBLANES
    ms = jnp.mean(x * x, axis=-1, keepdims=True)
    h = x * lax.rsqrt(ms + EPS) * pre_gain + shift
    hb = h.reshape(rows, D_MODEL).astype(BF16)
    for c in range(D_IN // MXU_DIM):
        cols = slice(c * MXU_DIM, (c + 1) * MXU_DIM)
        u_dst[:, cols] = jnp.dot(hb, win_ref[:, cols], preferred_element_type=F32)


def _stage_mix(u, x_view, o_view, post_gain, carry, chained, wout_ref, wca_ref, wcb_ref,
               bcb_ref, wr_ref, br_ref, wi_ref, bi_ref, lam_ref, wcc_ref, bcc_ref, lng_ref,
               lnb_ref, grp_ref, ea, eb, ec, hl_scr, p_scr, y_scr):
    na_ref, nb_ref, nh_ref, nc_ref = carry
    seg_len = x_view.shape[0]
    rows = seg_len * SUBLANES

    na = (CONV_A - 1) * SUBLANES
    ea[na:na + rows, :] = u[:, O_AC:O_AC + D_A] * u[:, O_AX:O_AX + D_A]
    _build_halo(ea, na_ref, CONV_A, seg_len, chained)
    ya = u[:, O_AB:O_AB + D_A] * _short_conv(ea, wca_ref, CONV_A, rows)
    na_ref[...] = ea[rows:rows + na, :]
    y_scr[0:rows, 0:D_A] = _rms(ya, grp_ref[:, 0:D_A]).astype(BF16)

    nb = (CONV_B - 1) * SUBLANES
    eb[nb:nb + rows, :] = u[:, O_BX:O_BX + D_B]
    _build_halo(eb, nb_ref, CONV_B, seg_len, chained)
    xb = _short_conv(eb, wcb_ref, CONV_B, rows) + bcb_ref[...]
    nb_ref[...] = eb[rows:rows + nb, :]
    xbb = xb.astype(BF16)
    half = D_B // 2
    gr = jnp.concatenate(
        [jnp.dot(xbb[:, :half], wr_ref[0], preferred_element_type=F32),
         jnp.dot(xbb[:, half:], wr_ref[1], preferred_element_type=F32)], axis=-1)
    gi = jnp.concatenate(
        [jnp.dot(xbb[:, :half], wi_ref[0], preferred_element_type=F32),
         jnp.dot(xbb[:, half:], wi_ref[1], preferred_element_type=F32)], axis=-1)
    r = jax.nn.sigmoid(gr + br_ref[...])
    ig = jax.nn.sigmoid(gi + bi_ref[...])
    log_a = RG_C * r * jax.nn.log_sigmoid(lam_ref[...])
    a = jnp.exp(log_a)
    b = jnp.sqrt(-jnp.tanh(log_a) * (a * a + 1.0)) * ig * xb

    hl = jnp.zeros((SUBLANES, D_B), F32)
    p = jnp.ones((SUBLANES, D_B), F32)
    for j in range(seg_len):
        grp_rows = slice(j * SUBLANES, (j + 1) * SUBLANES)
        aj = a[grp_rows, :]
        hl = aj * hl + b[grp_rows, :]
        p = aj * p
        hl_scr[grp_rows, :] = hl
        p_scr[grp_rows, :] = p

    prev_end = nh_ref[...]
    if chained:
        sub = lax.broadcasted_iota(jnp.int32, (SUBLANES, D_B), 0)
        ca, cb = p, hl
        for d in (1, 2, 4):
            a_sh = jnp.where(sub >= d, pltpu.roll(ca, d, axis=0), 1.0)
            b_sh = jnp.where(sub >= d, pltpu.roll(cb, d, axis=0), 0.0)
            cb = ca * b_sh + cb
            ca = ca * a_sh
        c0 = jnp.broadcast_to(prev_end[SUBLANES - 1:SUBLANES, :], (SUBLANES, D_B))
        seg_end = ca * c0 + cb
        seg_start = jnp.where(sub == 0, c0, pltpu.roll(seg_end, 1, axis=0))
    else:
        seg_start = prev_end
        seg_end = p * seg_start + hl
    nh_ref[...] = seg_end

    hl3 = hl_scr[0:rows, :].reshape(seg_len, SUBLANES, D_B)
    p3 = p_scr[0:rows, :].reshape(seg_len, SUBLANES, D_B)
    hseq = (hl3 + p3 * seg_start).reshape(rows, D_B)
    yb = hseq * jax.nn.gelu(u[:, O_BG:O_BG + D_B])
    y_scr[0:rows, D_A:D_A + D_B] = _rms(yb, grp_ref[:, D_A:D_A + D_B]).astype(BF16)

    nc = (CONV_C - 1) * SUBLANES
    ec[nc:nc + rows, :] = u[:, O_CV:O_CV + D_C] * jax.nn.sigmoid(u[:, O_CG:O_CG + D_C])
    _build_halo(ec, nc_ref, CONV_C, seg_len, chained)
    for blk in range(rows // CONV_ROWS):
        r0 = blk * CONV_ROWS
        acc = ec[r0:r0 + CONV_ROWS, :] * wcc_ref[0:1, :]
        for k in range(1, CONV_C):
            acc = acc + ec[r0 + k * SUBLANES:r0 + k * SUBLANES + CONV_ROWS, :] * wcc_ref[k:k + 1, :]
        yc = acc + bcc_ref[...]
        mu = jnp.mean(yc, axis=-1, keepdims=True)
        var = jnp.mean(jnp.square(yc - mu), axis=-1, keepdims=True)
        yc = (yc - mu) * lax.rsqrt(var + EPS) * lng_ref[...] + lnb_ref[...]
        yc = yc * jax.nn.sigmoid(yc)
        y_scr[r0:r0 + CONV_ROWS, D_A + D_B:] = _rms(yc, grp_ref[:, D_A + D_B:]).astype(BF16)
    nc_ref[...] = ec[rows:rows + nc, :]

    m = jnp.dot(y_scr[0:rows, :], wout_ref[...], preferred_element_type=F32)
    m = m.reshape(seg_len, SUBLANES, D_MODEL)
    ms = jnp.mean(m * m, axis=-1, keepdims=True)
    o_view[...] = x_view[...] + m * lax.rsqrt(ms + EPS) * post_gain


N_MIX_IN = 34


def _mixer_kernel(*refs, n_next):
    (x_ref, xn_ref, shift_ref, scale_ref, gate_ref,
     xs_ref, shift_s_ref, scale_s_ref, gate_s_ref, gpre_ref, gpost_ref, win_ref, wout_ref,
     wca_ref, wcb_ref, bcb_ref, wr_ref, br_ref, wi_ref, bi_ref, lam_ref,
     wcc_ref, bcc_ref, lng_ref, lnb_ref, grp_ref) = refs[:26]
    state_p = refs[26:30]
    state_s = refs[30:N_MIX_IN]
    next_src = refs[N_MIX_IN:N_MIX_IN + n_next]
    outs = refs[N_MIX_IN + n_next:]
    o_ref, carry_p, os_ref, carry_s = outs[0], outs[1:5], outs[5], outs[6:10]
    next_dst = outs[10:10 + n_next]
    u_scr, ea, eb, ec, hl_scr, p_scr, y_scr = outs[10 + n_next:]
    _cast_next_weights(next_src, next_dst)

    seg_len = x_ref.shape[0] // 2
    pre_gain = gpre_ref[...] * (1.0 + scale_ref[...])
    post_gain = gate_ref[...] * gpost_ref[...]
    shift = shift_ref[...]
    stage_in = functools.partial(_stage_in, win_ref=win_ref)
    stage_mix = functools.partial(
        _stage_mix, wout_ref=wout_ref, wca_ref=wca_ref, wcb_ref=wcb_ref, bcb_ref=bcb_ref,
        wr_ref=wr_ref, br_ref=br_ref, wi_ref=wi_ref, bi_ref=bi_ref, lam_ref=lam_ref,
        wcc_ref=wcc_ref, bcc_ref=bcc_ref, lng_ref=lng_ref, lnb_ref=lnb_ref, grp_ref=grp_ref,
        ea=ea, eb=eb, ec=ec, hl_scr=hl_scr, p_scr=p_scr, y_scr=y_scr)

    @pl.when(pl.program_id(0) == 0)
    def _():
        for dst, src in zip(carry_p + carry_s, state_p + state_s, strict=True):
            dst[...] = src[...]
        us = u_scr.at[1, pl.ds(0, xs_ref.shape[0] * SUBLANES)]
        stage_in(xs_ref[...], us, gpre_ref[...] * (1.0 + scale_s_ref[...]), shift_s_ref[...])
        stage_mix(us, xs_ref, os_ref, gate_s_ref[...] * gpost_ref[...], carry_s, False)
        stage_in(x_ref[0:seg_len], u_scr.at[0], pre_gain, shift)

    first, second = pl.ds(0, seg_len), pl.ds(seg_len, seg_len)
    stage_in(x_ref[seg_len:2 * seg_len], u_scr.at[1], pre_gain, shift)
    stage_mix(u_scr.at[0], x_ref.at[first], o_ref.at[first], post_gain, carry_p, True)
    stage_in(xn_ref[...], u_scr.at[0], pre_gain, shift)
    stage_mix(u_scr.at[1], x_ref.at[second], o_ref.at[second], post_gain, carry_p, True)


def _const_spec(shape, index):
    return pl.BlockSpec(shape, lambda i: index, pipeline_mode=pl.Buffered(1))


def _whole_out_spec(shape):
    return pl.BlockSpec(shape, lambda i: (0,) * len(shape))


def _mod_specs(layer, sub):
    return [_const_spec((None, SUBLANES, D_MODEL), (layer, 0, 3 * sub + k)) for k in range(3)]


def _norm_specs(layer, sub):
    return [_const_spec((None, 1, D_MODEL), (layer * N_SUB + sub, 0, 0))] * 2


def _next_weight_specs(next_w, steps):
    in_specs, out_specs, out_shapes = [], [], []
    for arr, lead in next_w:
        r, c = arr.shape[-2:]
        assert r % (steps * 2 * SUBLANES) == 0
        blk = r // steps
        in_specs.append(pl.BlockSpec((None,) * len(lead) + (blk, c),
                                     lambda i, lead=lead: lead + (i, 0)))
        out_specs.append(pl.BlockSpec((blk, c), lambda i: (i, 0)))
        out_shapes.append(jax.ShapeDtypeStruct((r, c), BF16))
    return in_specs, out_specs, out_shapes


def _ffn_call(xp, xs, mod_p, mod_s, gpre, gpost, w_bf16, next_w, layer, sub):
    groups = xp.shape[0]
    tl = FFN_GROUPS
    assert groups % tl == 0 and xs.shape[0] * SUBLANES <= tl * SUBLANES
    steps = groups // tl
    x_spec = pl.BlockSpec((tl, SUBLANES, D_MODEL), lambda i: (i, 0, 0))
    nin, nout, nshape = _next_weight_specs(next_w, steps)
    outs = pl.pallas_call(
        functools.partial(_ffn_kernel, n_next=len(next_w)),
        grid=(steps,),
        in_specs=[x_spec] + _mod_specs(layer, sub)
        + [_const_spec(xs.shape, (0, 0, 0))] + _mod_specs(layer, sub)
        + _norm_specs(layer, sub)
        + [_const_spec((D_MODEL, D_FF), (0, 0)), _const_spec((D_MODEL, D_FF), (0, 0)),
           _const_spec((D_FF, D_MODEL), (0, 0))] + nin,
        out_specs=[x_spec, _whole_out_spec(xs.shape)] + nout,
        out_shape=[jax.ShapeDtypeStruct(xp.shape, F32), jax.ShapeDtypeStruct(xs.shape, F32)]
        + nshape,
        scratch_shapes=[pltpu.VMEM((tl * SUBLANES, D_FF), BF16)],
        compiler_params=pltpu.CompilerParams(
            dimension_semantics=("arbitrary",), vmem_limit_bytes=VMEM_LIMIT),
        name=f"ffn_l{layer}_s{sub}",
    )(xp, mod_p, mod_p, mod_p, xs, mod_s, mod_s, mod_s, gpre, gpost, *w_bf16,
      *[arr for arr, _ in next_w])
    return outs[0], outs[1], tuple(outs[2:])


def _mixer_call(xp, xs, mod_p, mod_s, gpre, gpost, w_bf16, next_w, p, states_p, states_s, layer):
    seg_len = SEG_LEN
    n_tiles = xp.shape[0] // seg_len
    assert n_tiles % 2 == 0 and xs.shape[0] <= seg_len
    steps = n_tiles // 2
    rows = seg_len * SUBLANES
    sub = 1
    x_spec = pl.BlockSpec((2 * seg_len, SUBLANES, D_MODEL), lambda i: (i, 0, 0))
    next_spec = pl.BlockSpec((seg_len, SUBLANES, D_MODEL),
                             lambda i: (jnp.minimum(2 * i + 2, n_tiles - 1), 0, 0))

    def lspec(arr):
        shape = (None,) + arr.shape[1:]
        return _const_spec(shape, (layer,) + (0,) * (arr.ndim - 1))

    w_in, w_out = w_bf16
    weights = [p["w_conv_a"], p["w_conv_b"], p["b_conv_b"],
               p["w_gate_r"], p["b_gate_r"], p["w_gate_i"], p["b_gate_i"], p["rg_lambda"],
               p["w_conv_c"], p["b_conv_c"], p["ln_c_g"], p["ln_c_b"], p["grp_g"]]
    states = list(states_p) + list(states_s)
    nin, nout, nshape = _next_weight_specs(next_w, steps)
    in_specs = ([x_spec, next_spec] + _mod_specs(layer, sub)
                + [_const_spec(xs.shape, (0, 0, 0))] + _mod_specs(layer, sub)
                + _norm_specs(layer, sub)
                + [_const_spec(w_in.shape, (0, 0)), _const_spec(w_out.shape, (0, 0))]
                + [lspec(w) for w in weights] + [lspec(s) for s in states])
    assert len(in_specs) == N_MIX_IN
    outs = pl.pallas_call(
        functools.partial(_mixer_kernel, n_next=len(next_w)),
        grid=(steps,),
        in_specs=in_specs + nin,
        out_specs=[x_spec] + [_whole_out_spec(s.shape[1:]) for s in states_p]
        + [_whole_out_spec(xs.shape)] + [_whole_out_spec(s.shape[1:]) for s in states_s] + nout,
        out_shape=[jax.ShapeDtypeStruct(xp.shape, F32)]
        + [jax.ShapeDtypeStruct(s.shape[1:], F32) for s in states_p]
        + [jax.ShapeDtypeStruct(xs.shape, F32)]
        + [jax.ShapeDtypeStruct(s.shape[1:], F32) for s in states_s] + nshape,
        scratch_shapes=[
            pltpu.VMEM((2, rows, D_IN), F32),
            pltpu.VMEM((rows + (CONV_A - 1) * SUBLANES, D_A), F32),
            pltpu.VMEM((rows + (CONV_B - 1) * SUBLANES, D_B), F32),
            pltpu.VMEM((rows + (CONV_C - 1) * SUBLANES, D_C), F32),
            pltpu.VMEM((rows, D_B), F32),
            pltpu.VMEM((rows, D_B), F32),
            pltpu.VMEM((rows, D_MIX), BF16),
        ],
        compiler_params=pltpu.CompilerParams(
            dimension_semantics=("arbitrary",), vmem_limit_bytes=VMEM_LIMIT),
        name=f"mixer_l{layer}",
    )(xp, xp, mod_p, mod_p, mod_p, xs, mod_s, mod_s, mod_s, gpre, gpost, w_in, w_out,
      *weights, *states, *[arr for arr, _ in next_w])
    return outs[0], outs[5], outs[1:5], outs[6:10], tuple(outs[10:])


def _block_diag_halves(w):
    heads_per_half = MXU_DIM // HEAD_DIM_B
    w = w.reshape(DEPTH, 2, heads_per_half, HEAD_DIM_B, HEAD_DIM_B)
    eye = jnp.eye(heads_per_half, dtype=w.dtype)
    bd = jnp.einsum("lhaij,ab->lhaibj", w, eye)
    return bd.reshape(DEPTH, 2, MXU_DIM, MXU_DIM).astype(BF16)


def _trunk(xp, xs, mod_p, mod_s, params, states_p, states_s, w_f32):
    def f32_weights(key):
        if key[0] == "ffn":
            lead = key[1:]
            return [(w_f32["ffn_w1"], lead), (w_f32["ffn_w3"], lead), (w_f32["ffn_w2"], lead)]
        return [(w_f32["w_in"], key[1:]), (w_f32["w_out"], key[1:])]

    def cast_targets(key):
        if key[0] == "mix":
            return []
        if key[2] == 0:
            return [("mix", key[1]), ("ffn", key[1], 1)]
        return [("ffn", key[1] + 1, 0)] if key[1] + 1 < DEPTH else []

    order = []
    for l in range(DEPTH):
        order += [("ffn", l, 0), ("mix", l), ("ffn", l, 1)]
    w_bf16 = {order[0]: tuple(arr[lead].astype(BF16) for arr, lead in f32_weights(order[0]))}
    new_p, new_s = [], []
    gpre, gpost = params["norm_pre"], params["norm_post"]
    for key in order:
        targets = cast_targets(key)
        next_w = [w for t in targets for w in f32_weights(t)]
        l = key[1]
        if key[0] == "ffn":
            xp, xs, cast = _ffn_call(xp, xs, mod_p, mod_s, gpre, gpost, w_bf16[key], next_w, l,
                                     2 * key[2])
        else:
            xp, xs, ns_p, ns_s, cast = _mixer_call(xp, xs, mod_p, mod_s, gpre, gpost,
                                                   w_bf16[key], next_w, params, states_p,
                                                   states_s, l)
            new_p.append(ns_p)
            new_s.append(ns_s)
        for t in targets:
            n_w = len(f32_weights(t))
            w_bf16[t], cast = cast[:n_w], cast[n_w:]
    stack = lambda per_layer: [jnp.stack([ns[k] for ns in per_layer]) for k in range(4)]
    return xp, xs, stack(new_p), stack(new_s)


def kernel(x_prompt, x_sample, state_conv_a, state_conv_b, state_rglru, state_conv_c, c_prompt, c_sample, w_ada, b_ada, norm_pre, norm_post, ffn_w1, ffn_w3, ffn_w2, w_in, w_out, w_conv_a, w_conv_b, b_conv_b, w_gate_r, b_gate_r, w_gate_i, b_gate_i, rg_lambda, w_conv_c, b_conv_c, ln_c_g, ln_c_b, grp_g):
    batch, seq, _ = x_prompt.shape
    dec_batch, dec_seq, _ = x_sample.shape
    assert batch == 1 and dec_batch == SUBLANES
    assert seq % (SUBLANES * SEG_LEN) == 0 and SEG_LEN >= CONV_C - 1
    assert (dec_seq * SUBLANES) % CONV_ROWS == 0

    n_mod = 3 * N_SUB * D_MODEL
    c_all = jnp.concatenate(
        [c_prompt, c_sample, jnp.zeros((2 * SUBLANES - batch - dec_batch, D_MODEL), F32)], axis=0)
    ada_sub = ADA_K // ADA_SPLIT
    mod_all = pl.pallas_call(
        _ada_kernel,
        grid=(DEPTH, D_MODEL // ADA_K),
        in_specs=[pl.BlockSpec((2 * SUBLANES, ADA_K), lambda l, k: (0, k))]
        + [pl.BlockSpec((None, ada_sub, n_mod), lambda l, k, j=j: (l, ADA_SPLIT * k + j, 0))
           for j in range(ADA_SPLIT)]
        + [pl.BlockSpec((None, 1, n_mod), lambda l, k: (l, 0, 0))],
        out_specs=pl.BlockSpec((None, 2 * SUBLANES, n_mod), lambda l, k: (l, 0, 0)),
        out_shape=jax.ShapeDtypeStruct((DEPTH, 2 * SUBLANES, n_mod), F32),
        compiler_params=pltpu.CompilerParams(
            dimension_semantics=("parallel", "arbitrary"), vmem_limit_bytes=VMEM_LIMIT),
        name="adaln",
    )(c_all, *([w_ada] * ADA_SPLIT), b_ada.reshape(DEPTH, 1, n_mod))
    mod_p = jnp.broadcast_to(mod_all[:, 0:1], (DEPTH, SUBLANES, n_mod))
    mod_s = mod_all[:, 1:1 + SUBLANES]

    def row(v):
        return v.reshape(DEPTH, 1, v.shape[-1])

    params = {
        "norm_pre": norm_pre.reshape(DEPTH * N_SUB, 1, D_MODEL),
        "norm_post": norm_post.reshape(DEPTH * N_SUB, 1, D_MODEL),
        "w_conv_a": w_conv_a, "w_conv_b": w_conv_b, "b_conv_b": row(b_conv_b),
        "w_gate_r": _block_diag_halves(w_gate_r), "b_gate_r": row(b_gate_r),
        "w_gate_i": _block_diag_halves(w_gate_i), "b_gate_i": row(b_gate_i),
        "rg_lambda": row(rg_lambda),
        "w_conv_c": w_conv_c, "b_conv_c": row(b_conv_c),
        "ln_c_g": row(ln_c_g), "ln_c_b": row(ln_c_b), "grp_g": row(grp_g),
    }
    w_f32 = {"ffn_w1": ffn_w1, "ffn_w3": ffn_w3, "ffn_w2": ffn_w2, "w_in": w_in, "w_out": w_out}

    tiles = seq // (SUBLANES * SEG_LEN)
    xp = x_prompt.reshape(tiles, SUBLANES, SEG_LEN, D_MODEL).transpose(0, 2, 1, 3)
    xp = xp.reshape(tiles * SEG_LEN, SUBLANES, D_MODEL)
    zero_states = [jnp.zeros((DEPTH, (CONV_A - 1) * SUBLANES, D_A), F32),
                   jnp.zeros((DEPTH, (CONV_B - 1) * SUBLANES, D_B), F32),
                   jnp.zeros((DEPTH, SUBLANES, D_B), F32),
                   jnp.zeros((DEPTH, (CONV_C - 1) * SUBLANES, D_C), F32)]

    def to_rows(s):
        return s.transpose(0, 2, 1, 3).reshape(DEPTH, s.shape[2] * SUBLANES, s.shape[3])

    def from_rows(s, taps_m1):
        return s.reshape(DEPTH, taps_m1, SUBLANES, s.shape[-1]).transpose(0, 2, 1, 3)

    xs = x_sample.transpose(1, 0, 2)
    s_states = [to_rows(state_conv_a), to_rows(state_conv_b), state_rglru, to_rows(state_conv_c)]

    yp, ys, (pa, pb, ph, pc), (sa, sb, sh, sc) = _trunk(
        xp, xs, mod_p, mod_s, params, zero_states, s_states, w_f32)

    y_prompt = yp.reshape(tiles, SEG_LEN, SUBLANES, D_MODEL).transpose(0, 2, 1, 3)
    y_prompt = y_prompt.reshape(batch, seq, D_MODEL)

    def last_segment(s, taps_m1):
        return s.reshape(DEPTH, taps_m1, SUBLANES, s.shape[-1])[:, :, SUBLANES - 1][:, None]

    pa = last_segment(pa, CONV_A - 1)
    pb = last_segment(pb, CONV_B - 1)
    pc = last_segment(pc, CONV_C - 1)
    ph = ph[:, SUBLANES - 1:SUBLANES]

    y_sample = ys.transpose(1, 0, 2)
    sa = from_rows(sa, CONV_A - 1)
    sb = from_rows(sb, CONV_B - 1)
    sc = from_rows(sc, CONV_C - 1)

    return (y_prompt, y_sample, pa, pb, ph, pc, sa, sb, sh, sc)
```

```python
import functools

import jax
import jax.numpy as jnp
from jax import lax
from jax.experimental import pallas as pl
from jax.experimental.pallas import tpu as pltpu

D_MODEL = 1024
DEPTH = 4
D_A = 256
D_B = 512
HEADS_B = 8
HEAD_DIM_B = D_B // HEADS_B
D_C = 256
D_MIX = D_A + D_B + D_C
D_IN = 3 * D_A + 2 * D_B + 2 * D_C
CONV_A = 3
CONV_B = 4
CONV_C = 31
RG_C = 8.0
D_FF = 2816
N_SUB = 3
EPS = 1e-6

SUBLANES = 8
LANES = 128
MXU_DIM = 256
SEG_LEN = 64
FF_CHUNK = MXU_DIM
CONV_ROWS = 64
ADA_K = MXU_DIM
FFN_GROUPS = 128
FFN_SUB_GROUPS = 32
VMEM_LIMIT = 56 * 1024 * 1024

O_AB, O_AC, O_AX = 0, D_A, 2 * D_A
O_BX = 3 * D_A
O_BG = O_BX + D_B
O_CV = O_BG + D_B
O_CG = O_CV + D_C

F32 = jnp.float32
BF16 = jnp.bfloat16


def _rms(x, g):
    ms = jnp.mean(x * x, axis=-1, keepdims=True)
    return x * lax.rsqrt(ms + EPS) * g


def _ada_kernel(c_ref, w_ref, b_ref, o_ref):
    k = pl.program_id(1)
    c = c_ref[...]
    s = (c * jax.nn.sigmoid(c)).astype(BF16)
    part = jnp.dot(s, w_ref[...].astype(BF16), preferred_element_type=F32)

    @pl.when(k == 0)
    def _():
        o_ref[...] = part + b_ref[...]

    @pl.when(k > 0)
    def _():
        o_ref[...] += part


def _cast_next_weights(src_refs, dst_refs):
    for src, dst in zip(src_refs, dst_refs, strict=True):
        dst[...] = src[...].astype(BF16)


def _ffn_rows(x, pre_gain, shift, post_gain, w1_ref, w3_ref, w2_ref, act):
    groups = x.shape[0]
    rows = groups * SUBLANES
    ms = jnp.mean(x * x, axis=-1, keepdims=True)
    h = x * lax.rsqrt(ms + EPS) * pre_gain + shift
    hb = h.reshape(rows, D_MODEL).astype(BF16)
    for c in range(D_FF // FF_CHUNK):
        cols = slice(c * FF_CHUNK, (c + 1) * FF_CHUNK)
        g = jnp.dot(hb, w1_ref[:, cols], preferred_element_type=F32)
        v = jnp.dot(hb, w3_ref[:, cols], preferred_element_type=F32)
        act[:, cols] = (g * jax.nn.sigmoid(g) * v).astype(BF16)
    f = jnp.dot(act[...], w2_ref[...], preferred_element_type=F32)
    f = f.reshape(groups, SUBLANES, D_MODEL)
    ms = jnp.mean(f * f, axis=-1, keepdims=True)
    return x + f * lax.rsqrt(ms + EPS) * post_gain


def _store_natural(res, o_ref, il_scr, group0):
    groups = res.shape[0]
    assert SEG_LEN % groups == 0
    res2 = res.reshape(groups * SUBLANES, D_MODEL)
    tile, j0 = divmod(group0, SEG_LEN)
    for c in range(D_MODEL // LANES):
        il_scr[c] = res2[:, c * LANES:(c + 1) * LANES]
    for seg in range(SUBLANES):
        r0 = tile * SUBLANES * SEG_LEN + seg * SEG_LEN + j0
        for c in range(D_MODEL // LANES):
            o_ref[r0:r0 + groups, c * LANES:(c + 1) * LANES] = (
                il_scr[c, pl.ds(seg, groups, stride=SUBLANES), :])


def _ffn_kernel(*refs, n_next, natural_out):
    (x_ref, shift_ref, scale_ref, gate_ref, xs_ref, shift_s_ref, scale_s_ref, gate_s_ref,
     gpre_ref, gpost_ref, w1_ref, w3_ref, w2_ref) = refs[:13]
    next_src = refs[13:13 + n_next]
    o_ref, os_ref = refs[13 + n_next:15 + n_next]
    next_dst = refs[15 + n_next:15 + 2 * n_next]
    act_scr = refs[15 + 2 * n_next]
    il_scr = refs[16 + 2 * n_next] if natural_out else None
    _cast_next_weights(next_src, next_dst)
    rows_fn = functools.partial(_ffn_rows, w1_ref=w1_ref, w3_ref=w3_ref, w2_ref=w2_ref)

    @pl.when(pl.program_id(0) == 0)
    def _():
        rows_s = xs_ref.shape[0] * SUBLANES
        os_ref[...] = rows_fn(xs_ref[...], gpre_ref[...] * (1.0 + scale_s_ref[...]),
                              shift_s_ref[...], 0.5 * gate_s_ref[...] * gpost_ref[...],
                              act=act_scr.at[pl.ds(0, rows_s)])

    tl = x_ref.shape[0]
    sub_tl = min(tl, FFN_SUB_GROUPS)
    sub_rows = sub_tl * SUBLANES
    pre_gain = gpre_ref[...] * (1.0 + scale_ref[...])
    post_gain = 0.5 * gate_ref[...] * gpost_ref[...]
    shift = shift_ref[...]
    for s in range(tl // sub_tl):
        grp = slice(s * sub_tl, (s + 1) * sub_tl)
        res = rows_fn(x_ref[grp], pre_gain, shift, post_gain,
                      act=act_scr.at[pl.ds(s * sub_rows, sub_rows)])
        if natural_out:
            _store_natural(res, o_ref, il_scr, s * sub_tl)
        else:
            o_ref[grp] = res


def _build_halo(e_ref, prev_ref, taps, seg_len, chained):
    n = (taps - 1) * SUBLANES
    prev = prev_ref[...]
    if chained:
        cur = e_ref[seg_len * SUBLANES:seg_len * SUBLANES + n, :]
        sub = lax.broadcasted_iota(jnp.int32, cur.shape, 0) & (SUBLANES - 1)
        halo = jnp.where(sub == 0, pltpu.roll(prev, n - (SUBLANES - 1), axis=0),
                         pltpu.roll(cur, 1, axis=0))
    else:
        halo = prev
    e_ref[0:n, :] = halo


def _short_conv(e_ref, w_ref, taps, rows):
    acc = e_ref[0:rows, :] * w_ref[0:1, :]
    for k in range(1, taps):
        acc = acc + e_ref[k * SUBLANES:k * SUBLANES + rows, :] * w_ref[k:k + 1, :]
    return acc


def _stage_in_steps(x, u_dst, pre_gain, shift, win_ref):
    rows = x.shape[0] * SUBLANES
    ms = jnp.mean(x * x, axis=-1, keepdims=True)
    h = x * lax.rsqrt(ms + EPS) * pre_gain + shift
    hb = h.reshape(rows, D_MODEL).astype(BF16)
    for c in range(D_IN // MXU_DIM):
        cols = slice(c * MXU_DIM, (c + 1) * MXU_DIM)
        u_dst[:, cols] = jnp.dot(hb, win_ref[:, cols], preferred_element_type=F32)
        yield


def _stage_in(*args, **kwargs):
    for _ in _stage_in_steps(*args, **kwargs):
        pass


def _no_tick(n=1):
    del n


def _stage_mix(u, x_view, o_view, post_gain, carry, chained, tick, wout_ref, wca_ref, wcb_ref,
               bcb_ref, wr_ref, br_ref, wi_ref, bi_ref, lam_ref, wcc_ref, bcc_ref, lng_ref,
               lnb_ref, grp_ref, ea, eb, ec, hl_scr, p_scr, y_scr):
    na_ref, nb_ref, nh_ref, nc_ref = carry
    seg_len = x_view.shape[0]
    rows = seg_len * SUBLANES

    tick(2)
    na = (CONV_A - 1) * SUBLANES
    ea[na:na + rows, :] = u[:, O_AC:O_AC + D_A] * u[:, O_AX:O_AX + D_A]
    _build_halo(ea, na_ref, CONV_A, seg_len, chained)
    ya = u[:, O_AB:O_AB + D_A] * _short_conv(ea, wca_ref, CONV_A, rows)
    na_ref[...] = ea[rows:rows + na, :]
    y_scr[0:rows, 0:D_A] = _rms(ya, grp_ref[:, 0:D_A]).astype(BF16)

    tick()
    nb = (CONV_B - 1) * SUBLANES
    eb[nb:nb + rows, :] = u[:, O_BX:O_BX + D_B]
    _build_halo(eb, nb_ref, CONV_B, seg_len, chained)
    xb = _short_conv(eb, wcb_ref, CONV_B, rows) + bcb_ref[...]
    nb_ref[...] = eb[rows:rows + nb, :]
    xbb = xb.astype(BF16)
    half = D_B // 2
    gr = jnp.concatenate(
        [jnp.dot(xbb[:, :half], wr_ref[0], preferred_element_type=F32),
         jnp.dot(xbb[:, half:], wr_ref[1], preferred_element_type=F32)], axis=-1)
    gi = jnp.concatenate(
        [jnp.dot(xbb[:, :half], wi_ref[0], preferred_element_type=F32),
         jnp.dot(xbb[:, half:], wi_ref[1], preferred_element_type=F32)], axis=-1)
    tick()
    r = jax.nn.sigmoid(gr + br_ref[...])
    ig = jax.nn.sigmoid(gi + bi_ref[...])
    log_a = RG_C * r * jax.nn.log_sigmoid(lam_ref[...])
    a = jnp.exp(log_a)
    b = jnp.sqrt(-jnp.tanh(log_a) * (a * a + 1.0)) * ig * xb

    tick()
    hl = jnp.zeros((SUBLANES, D_B), F32)
    p = jnp.ones((SUBLANES, D_B), F32)
    for j in range(seg_len):
        grp_rows = slice(j * SUBLANES, (j + 1) * SUBLANES)
        aj = a[grp_rows, :]
        hl = aj * hl + b[grp_rows, :]
        p = aj * p
        hl_scr[grp_rows, :] = hl
        p_scr[grp_rows, :] = p

    prev_end = nh_ref[...]
    if chained:
        sub = lax.broadcasted_iota(jnp.int32, (SUBLANES, D_B), 0)
        ca, cb = p, hl
        for d in (1, 2, 4):
            a_sh = jnp.where(sub >= d, pltpu.roll(ca, d, axis=0), 1.0)
            b_sh = jnp.where(sub >= d, pltpu.roll(cb, d, axis=0), 0.0)
            cb = ca * b_sh + cb
            ca = ca * a_sh
        c0 = jnp.broadcast_to(prev_end[SUBLANES - 1:SUBLANES, :], (SUBLANES, D_B))
        seg_end = ca * c0 + cb
        seg_start = jnp.where(sub == 0, c0, pltpu.roll(seg_end, 1, axis=0))
    else:
        seg_start = prev_end
        seg_end = p * seg_start + hl
    nh_ref[...] = seg_end

    tick()
    hl3 = hl_scr[0:rows, :].reshape(seg_len, SUBLANES, D_B)
    p3 = p_scr[0:rows, :].reshape(seg_len, SUBLANES, D_B)
    hseq = (hl3 + p3 * seg_start).reshape(rows, D_B)
    yb = hseq * jax.nn.gelu(u[:, O_BG:O_BG + D_B])
    y_scr[0:rows, D_A:D_A + D_B] = _rms(yb, grp_ref[:, D_A:D_A + D_B]).astype(BF16)

    nc = (CONV_C - 1) * SUBLANES
    ec[nc:nc + rows, :] = u[:, O_CV:O_CV + D_C] * jax.nn.sigmoid(u[:, O_CG:O_CG + D_C])
    _build_halo(ec, nc_ref, CONV_C, seg_len, chained)
    for blk in range(rows // CONV_ROWS):
        r0 = blk * CONV_ROWS
        acc = ec[r0:r0 + CONV_ROWS, :] * wcc_ref[0:1, :]
        for k in range(1, CONV_C):
            acc = acc + ec[r0 + k * SUBLANES:r0 + k * SUBLANES + CONV_ROWS, :] * wcc_ref[k:k + 1, :]
        yc = acc + bcc_ref[...]
        mu = jnp.mean(yc, axis=-1, keepdims=True)
        var = jnp.mean(jnp.square(yc - mu), axis=-1, keepdims=True)
        yc = (yc - mu) * lax.rsqrt(var + EPS) * lng_ref[...] + lnb_ref[...]
        yc = yc * jax.nn.sigmoid(yc)
        y_scr[r0:r0 + CONV_ROWS, D_A + D_B:] = _rms(yc, grp_ref[:, D_A + D_B:]).astype(BF16)
        if blk % 2 == 1:
            tick()
    nc_ref[...] = ec[rows:rows + nc, :]

    tick(D_IN // MXU_DIM)
    m = jnp.dot(y_scr[0:rows, :], wout_ref[...], preferred_element_type=F32)
    m = m.reshape(seg_len, SUBLANES, D_MODEL)
    ms = jnp.mean(m * m, axis=-1, keepdims=True)
    o_view[...] = x_view[...] + m * lax.rsqrt(ms + EPS) * post_gain


N_MIX_IN = 34


def _mixer_kernel(*refs, n_next):
    (x_ref, xn_ref, shift_ref, scale_ref, gate_ref,
     xs_ref, shift_s_ref, scale_s_ref, gate_s_ref, gpre_ref, gpost_ref, win_ref, wout_ref,
     wca_ref, wcb_ref, bcb_ref, wr_ref, br_ref, wi_ref, bi_ref, lam_ref,
     wcc_ref, bcc_ref, lng_ref, lnb_ref, grp_ref) = refs[:26]
    state_p = refs[26:30]
    state_s = refs[30:N_MIX_IN]
    next_src = refs[N_MIX_IN:N_MIX_IN + n_next]
    outs = refs[N_MIX_IN + n_next:]
    o_ref, carry_p, os_ref, carry_s = outs[0], outs[1:5], outs[5], outs[6:10]
    next_dst = outs[10:10 + n_next]
    u_scr, ea, eb, ec, hl_scr, p_scr, y_scr = outs[10 + n_next:]
    _cast_next_weights(next_src, next_dst)

    seg_len = x_ref.shape[0] // 2
    pre_gain = gpre_ref[...] * (1.0 + scale_ref[...])
    post_gain = gate_ref[...] * gpost_ref[...]
    shift = shift_ref[...]
    stage_in = functools.partial(_stage_in, win_ref=win_ref)
    stage_mix = functools.partial(
        _stage_mix, wout_ref=wout_ref, wca_ref=wca_ref, wcb_ref=wcb_ref, bcb_ref=bcb_ref,
        wr_ref=wr_ref, br_ref=br_ref, wi_ref=wi_ref, bi_ref=bi_ref, lam_ref=lam_ref,
        wcc_ref=wcc_ref, bcc_ref=bcc_ref, lng_ref=lng_ref, lnb_ref=lnb_ref, grp_ref=grp_ref,
        ea=ea, eb=eb, ec=ec, hl_scr=hl_scr, p_scr=p_scr, y_scr=y_scr)

    @pl.when(pl.program_id(0) == 0)
    def _():
        for dst, src in zip(carry_p + carry_s, state_p + state_s, strict=True):
            dst[...] = src[...]
        us = u_scr.at[1, pl.ds(0, xs_ref.shape[0] * SUBLANES)]
        stage_in(xs_ref[...], us, gpre_ref[...] * (1.0 + scale_s_ref[...]), shift_s_ref[...])
        stage_mix(us, xs_ref, os_ref, gate_s_ref[...] * gpost_ref[...], carry_s, False,
                  _no_tick)
        stage_in(x_ref[0:seg_len], u_scr.at[0], pre_gain, shift)

    def ticker(x_next, u_dst):
        steps = _stage_in_steps(x_next, u_dst, pre_gain, shift, win_ref)

        def tick(n=1):
            for _ in range(n):
                next(steps, None)
        return tick

    first, second = pl.ds(0, seg_len), pl.ds(seg_len, seg_len)
    stage_mix(u_scr.at[0], x_ref.at[first], o_ref.at[first], post_gain, carry_p, True,
              ticker(x_ref[seg_len:2 * seg_len], u_scr.at[1]))
    stage_mix(u_scr.at[1], x_ref.at[second], o_ref.at[second], post_gain, carry_p, True,
              ticker(xn_ref[...], u_scr.at[0]))


def _const_spec(shape, index):
    return pl.BlockSpec(shape, lambda i: index, pipeline_mode=pl.Buffered(1))


def _whole_out_spec(shape):
    return pl.BlockSpec(shape, lambda i: (0,) * len(shape))


def _mod_specs(layer, sub):
    return [_const_spec((None, SUBLANES, D_MODEL), (layer, 0, 3 * sub + k)) for k in range(3)]


def _norm_specs(layer, sub):
    return [_const_spec((None, 1, D_MODEL), (layer * N_SUB + sub, 0, 0))] * 2


def _next_weight_specs(next_w, steps):
    in_specs, out_specs, out_shapes = [], [], []
    for arr, lead in next_w:
        r, c = arr.shape[-2:]
        assert r % (steps * 2 * SUBLANES) == 0
        blk = r // steps
        in_specs.append(pl.BlockSpec((None,) * len(lead) + (blk, c),
                                     lambda i, lead=lead: lead + (i, 0)))
        out_specs.append(pl.BlockSpec((blk, c), lambda i: (i, 0)))
        out_shapes.append(jax.ShapeDtypeStruct((r, c), BF16))
    return in_specs, out_specs, out_shapes


def _ffn_call(xp, xs, mod_p, mod_s, gpre, gpost, w_bf16, next_w, layer, sub, natural_out):
    groups = xp.shape[0]
    tl = FFN_GROUPS
    assert groups % tl == 0 and xs.shape[0] * SUBLANES <= tl * SUBLANES
    steps = groups // tl
    x_spec = pl.BlockSpec((tl, SUBLANES, D_MODEL), lambda i: (i, 0, 0))
    nin, nout, nshape = _next_weight_specs(next_w, steps)
    if natural_out:
        assert tl % SEG_LEN == 0
        o_spec = pl.BlockSpec((tl * SUBLANES, D_MODEL), lambda i: (i, 0))
        o_shape = jax.ShapeDtypeStruct((groups * SUBLANES, D_MODEL), F32)
        scratch = [pltpu.VMEM((D_MODEL // LANES, FFN_SUB_GROUPS * SUBLANES, LANES), F32)]
    else:
        o_spec, o_shape, scratch = x_spec, jax.ShapeDtypeStruct(xp.shape, F32), []
    outs = pl.pallas_call(
        functools.partial(_ffn_kernel, n_next=len(next_w), natural_out=natural_out),
        grid=(steps,),
        in_specs=[x_spec] + _mod_specs(layer, sub)
        + [_const_spec(xs.shape, (0, 0, 0))] + _mod_specs(layer, sub)
        + _norm_specs(layer, sub)
        + [_const_spec((D_MODEL, D_FF), (0, 0)), _const_spec((D_MODEL, D_FF), (0, 0)),
           _const_spec((D_FF, D_MODEL), (0, 0))] + nin,
        out_specs=[o_spec, _whole_out_spec(xs.shape)] + nout,
        out_shape=[o_shape, jax.ShapeDtypeStruct(xs.shape, F32)] + nshape,
        scratch_shapes=[pltpu.VMEM((tl * SUBLANES, D_FF), BF16)] + scratch,
        compiler_params=pltpu.CompilerParams(
            dimension_semantics=("arbitrary",), vmem_limit_bytes=VMEM_LIMIT),
        name=f"ffn_l{layer}_s{sub}",
    )(xp, mod_p, mod_p, mod_p, xs, mod_s, mod_s, mod_s, gpre, gpost, *w_bf16,
      *[arr for arr, _ in next_w])
    return outs[0], outs[1], tuple(outs[2:])


def _mixer_call(xp, xs, mod_p, mod_s, gpre, gpost, w_bf16, next_w, p, states_p, states_s, layer):
    seg_len = SEG_LEN
    n_tiles = xp.shape[0] // seg_len
    assert n_tiles % 2 == 0 and xs.shape[0] <= seg_len
    steps = n_tiles // 2
    rows = seg_len * SUBLANES
    sub = 1
    x_spec = pl.BlockSpec((2 * seg_len, SUBLANES, D_MODEL), lambda i: (i, 0, 0))
    next_spec = pl.BlockSpec((seg_len, SUBLANES, D_MODEL),
                             lambda i: (jnp.minimum(2 * i + 2, n_tiles - 1), 0, 0))

    def lspec(arr):
        shape = (None,) + arr.shape[1:]
        return _const_spec(shape, (layer,) + (0,) * (arr.ndim - 1))

    w_in, w_out = w_bf16
    weights = [p["w_conv_a"], p["w_conv_b"], p["b_conv_b"],
               p["w_gate_r"], p["b_gate_r"], p["w_gate_i"], p["b_gate_i"], p["rg_lambda"],
               p["w_conv_c"], p["b_conv_c"], p["ln_c_g"], p["ln_c_b"], p["grp_g"]]
    states = list(states_p) + list(states_s)
    nin, nout, nshape = _next_weight_specs(next_w, steps)
    in_specs = ([x_spec, next_spec] + _mod_specs(layer, sub)
                + [_const_spec(xs.shape, (0, 0, 0))] + _mod_specs(layer, sub)
                + _norm_specs(layer, sub)
                + [_const_spec(w_in.shape, (0, 0)), _const_spec(w_out.shape, (0, 0))]
                + [lspec(w) for w in weights] + [lspec(s) for s in states])
    assert len(in_specs) == N_MIX_IN
    outs = pl.pallas_call(
        functools.partial(_mixer_kernel, n_next=len(next_w)),
        grid=(steps,),
        in_specs=in_specs + nin,
        out_specs=[x_spec] + [_whole_out_spec(s.shape[1:]) for s in states_p]
        + [_whole_out_spec(xs.shape)] + [_whole_out_spec(s.shape[1:]) for s in states_s] + nout,
        out_shape=[jax.ShapeDtypeStruct(xp.shape, F32)]
        + [jax.ShapeDtypeStruct(s.shape[1:], F32) for s in states_p]
        + [jax.ShapeDtypeStruct(xs.shape, F32)]
        + [jax.ShapeDtypeStruct(s.shape[1:], F32) for s in states_s] + nshape,
        scratch_shapes=[
            pltpu.VMEM((2, rows, D_IN), F32),
            pltpu.VMEM((rows + (CONV_A - 1) * SUBLANES, D_A), F32),
            pltpu.VMEM((rows + (CONV_B - 1) * SUBLANES, D_B), F32),
            pltpu.VMEM((rows + (CONV_C - 1) * SUBLANES, D_C), F32),
            pltpu.VMEM((rows, D_B), F32),
            pltpu.VMEM((rows, D_B), F32),
            pltpu.VMEM((rows, D_MIX), BF16),
        ],
        compiler_params=pltpu.CompilerParams(
            dimension_semantics=("arbitrary",), vmem_limit_bytes=VMEM_LIMIT),
        name=f"mixer_l{layer}",
    )(xp, xp, mod_p, mod_p, mod_p, xs, mod_s, mod_s, mod_s, gpre, gpost, w_in, w_out,
      *weights, *states, *[arr for arr, _ in next_w])
    return outs[0], outs[5], outs[1:5], outs[6:10], tuple(outs[10:])


def _block_diag_halves(w):
    heads_per_half = MXU_DIM // HEAD_DIM_B
    w = w.reshape(DEPTH, 2, heads_per_half, HEAD_DIM_B, HEAD_DIM_B)
    eye = jnp.eye(heads_per_half, dtype=w.dtype)
    bd = jnp.einsum("lhaij,ab->lhaibj", w, eye)
    return bd.reshape(DEPTH, 2, MXU_DIM, MXU_DIM).astype(BF16)


def _trunk(xp, xs, mod_p, mod_s, params, states_p, states_s, w_f32):
    def f32_weights(key):
        if key[0] == "ffn":
            lead = key[1:]
            return [(w_f32["ffn_w1"], lead), (w_f32["ffn_w3"], lead), (w_f32["ffn_w2"], lead)]
        return [(w_f32["w_in"], key[1:]), (w_f32["w_out"], key[1:])]

    def cast_targets(key):
        if key[0] == "mix":
            return []
        if key[2] == 0:
            return [("mix", key[1]), ("ffn", key[1], 1)]
        return [("ffn", key[1] + 1, 0)] if key[1] + 1 < DEPTH else []

    order = []
    for l in range(DEPTH):
        order += [("ffn", l, 0), ("mix", l), ("ffn", l, 1)]
    w_bf16 = {order[0]: tuple(arr[lead].astype(BF16) for arr, lead in f32_weights(order[0]))}
    new_p, new_s = [], []
    gpre, gpost = params["norm_pre"], params["norm_post"]
    for key in order:
        targets = cast_targets(key)
        next_w = [w for t in targets for w in f32_weights(t)]
        l = key[1]
        if key[0] == "ffn":
            xp, xs, cast = _ffn_call(xp, xs, mod_p, mod_s, gpre, gpost, w_bf16[key], next_w, l,
                                     2 * key[2], natural_out=key == order[-1])
        else:
            xp, xs, ns_p, ns_s, cast = _mixer_call(xp, xs, mod_p, mod_s, gpre, gpost,
                                                   w_bf16[key], next_w, params, states_p,
                                                   states_s, l)
            new_p.append(ns_p)
            new_s.append(ns_s)
        for t in targets:
            n_w = len(f32_weights(t))
            w_bf16[t], cast = cast[:n_w], cast[n_w:]
    stack = lambda per_layer: [jnp.stack([ns[k] for ns in per_layer]) for k in range(4)]
    return xp, xs, stack(new_p), stack(new_s)


def kernel(x_prompt, x_sample, state_conv_a, state_conv_b, state_rglru, state_conv_c, c_prompt, c_sample, w_ada, b_ada, norm_pre, norm_post, ffn_w1, ffn_w3, ffn_w2, w_in, w_out, w_conv_a, w_conv_b, b_conv_b, w_gate_r, b_gate_r, w_gate_i, b_gate_i, rg_lambda, w_conv_c, b_conv_c, ln_c_g, ln_c_b, grp_g):
    batch, seq, _ = x_prompt.shape
    dec_batch, dec_seq, _ = x_sample.shape
    assert batch == 1 and dec_batch == SUBLANES
    assert seq % (SUBLANES * SEG_LEN) == 0 and SEG_LEN >= CONV_C - 1
    assert (dec_seq * SUBLANES) % CONV_ROWS == 0

    n_mod = 3 * N_SUB * D_MODEL
    c_all = jnp.concatenate(
        [c_prompt, c_sample, jnp.zeros((2 * SUBLANES - batch - dec_batch, D_MODEL), F32)], axis=0)
    mod_all = pl.pallas_call(
        _ada_kernel,
        grid=(DEPTH, D_MODEL // ADA_K),
        in_specs=[pl.BlockSpec((2 * SUBLANES, ADA_K), lambda l, k: (0, k)),
                  pl.BlockSpec((None, ADA_K, n_mod), lambda l, k: (l, k, 0)),
                  pl.BlockSpec((None, 1, n_mod), lambda l, k: (l, 0, 0))],
        out_specs=pl.BlockSpec((None, 2 * SUBLANES, n_mod), lambda l, k: (l, 0, 0)),
        out_shape=jax.ShapeDtypeStruct((DEPTH, 2 * SUBLANES, n_mod), F32),
        compiler_params=pltpu.CompilerParams(
            dimension_semantics=("parallel", "arbitrary"), vmem_limit_bytes=VMEM_LIMIT),
        name="adaln",
    )(c_all, w_ada, b_ada.reshape(DEPTH, 1, n_mod))
    mod_p = jnp.broadcast_to(mod_all[:, 0:1], (DEPTH, SUBLANES, n_mod))
    mod_s = mod_all[:, 1:1 + SUBLANES]

    def row(v):
        return v.reshape(DEPTH, 1, v.shape[-1])

    params = {
        "norm_pre": norm_pre.reshape(DEPTH * N_SUB, 1, D_MODEL),
        "norm_post": norm_post.reshape(DEPTH * N_SUB, 1, D_MODEL),
        "w_conv_a": w_conv_a, "w_conv_b": w_conv_b, "b_conv_b": row(b_conv_b),
        "w_gate_r": _block_diag_halves(w_gate_r), "b_gate_r": row(b_gate_r),
        "w_gate_i": _block_diag_halves(w_gate_i), "b_gate_i": row(b_gate_i),
        "rg_lambda": row(rg_lambda),
        "w_conv_c": w_conv_c, "b_conv_c": row(b_conv_c),
        "ln_c_g": row(ln_c_g), "ln_c_b": row(ln_c_b), "grp_g": row(grp_g),
    }
    w_f32 = {"ffn_w1": ffn_w1, "ffn_w3": ffn_w3, "ffn_w2": ffn_w2, "w_in": w_in, "w_out": w_out}

    tiles = seq // (SUBLANES * SEG_LEN)
    xp = x_prompt.reshape(tiles, SUBLANES, SEG_LEN, D_MODEL).transpose(0, 2, 1, 3)
    xp = xp.reshape(tiles * SEG_LEN, SUBLANES, D_MODEL)
    zero_states = [jnp.zeros((DEPTH, (CONV_A - 1) * SUBLANES, D_A), F32),
                   jnp.zeros((DEPTH, (CONV_B - 1) * SUBLANES, D_B), F32),
                   jnp.zeros((DEPTH, SUBLANES, D_B), F32),
                   jnp.zeros((DEPTH, (CONV_C - 1) * SUBLANES, D_C), F32)]

    def to_rows(s):
        return s.transpose(0, 2, 1, 3).reshape(DEPTH, s.shape[2] * SUBLANES, s.shape[3])

    def from_rows(s, taps_m1):
        return s.reshape(DEPTH, taps_m1, SUBLANES, s.shape[-1]).transpose(0, 2, 1, 3)

    xs = x_sample.transpose(1, 0, 2)
    s_states = [to_rows(state_conv_a), to_rows(state_conv_b), state_rglru, to_rows(state_conv_c)]

    yp, ys, (pa, pb, ph, pc), (sa, sb, sh, sc) = _trunk(
        xp, xs, mod_p, mod_s, params, zero_states, s_states, w_f32)

    y_prompt = yp.reshape(batch, seq, D_MODEL)

    def last_segment(s, taps_m1):
        return s.reshape(DEPTH, taps_m1, SUBLANES, s.shape[-1])[:, :, SUBLANES - 1][:, None]

    pa = last_segment(pa, CONV_A - 1)
    pb = last_segment(pb, CONV_B - 1)
    pc = last_segment(pc, CONV_C - 1)
    ph = ph[:, SUBLANES - 1:SUBLANES]

    y_sample = ys.transpose(1, 0, 2)
    sa = from_rows(sa, CONV_A - 1)
    sb = from_rows(sb, CONV_B - 1)
    sc = from_rows(sc, CONV_C - 1)

    return (y_prompt, y_sample, pa, pb, ph, pc, sa, sb, sh, sc)
```

```python
import functools

import jax
import jax.numpy as jnp
from jax import lax
from jax.experimental import pallas as pl
from jax.experimental.pallas import tpu as pltpu

D_MODEL = 1024
DEPTH = 4
D_A = 256
D_B = 512
HEADS_B = 8
HEAD_DIM_B = D_B // HEADS_B
D_C = 256
D_MIX = D_A + D_B + D_C
D_IN = 3 * D_A + 2 * D_B + 2 * D_C
CONV_A = 3
CONV_B = 4
CONV_C = 31
RG_C = 8.0
D_FF = 2816
N_SUB = 3
EPS = 1e-6

SUBLANES = 8
LANES = 128
MXU_DIM = 256
SEG_LEN = 64
FF_CHUNK = MXU_DIM
CONV_ROWS = 64
ADA_K = MXU_DIM
FFN_GROUPS = 128
FFN_SUB_GROUPS = 32
VMEM_LIMIT = 56 * 1024 * 1024

O_AB, O_AC, O_AX = 0, D_A, 2 * D_A
O_BX = 3 * D_A
O_BG = O_BX + D_B
O_CV = O_BG + D_B
O_CG = O_CV + D_C

F32 = jnp.float32
BF16 = jnp.bfloat16


def _rms(x, g):
    ms = jnp.mean(x * x, axis=-1, keepdims=True)
    return x * lax.rsqrt(ms + EPS) * g


def _ada_step(c_ref, w_ref, b_ref, o_ref):
    k = pl.program_id(0)
    c = c_ref[...]
    s = (c * jax.nn.sigmoid(c)).astype(BF16)
    part = jnp.dot(s, w_ref[...].astype(BF16), preferred_element_type=F32)

    @pl.when(k == 0)
    def _():
        o_ref[...] = part + b_ref[...]

    @pl.when(k > 0)
    def _():
        o_ref[...] += part


def _cast_next_weights(src_refs, dst_refs):
    for src, dst in zip(src_refs, dst_refs, strict=True):
        dst[...] = src[...].astype(BF16)


def _ffn_rows(x, pre_gain, shift, post_gain, w1_ref, w3_ref, w2_ref, act):
    groups = x.shape[0]
    rows = groups * SUBLANES
    ms = jnp.mean(x * x, axis=-1, keepdims=True)
    h = x * lax.rsqrt(ms + EPS) * pre_gain + shift
    hb = h.reshape(rows, D_MODEL).astype(BF16)
    for c in range(D_FF // FF_CHUNK):
        cols = slice(c * FF_CHUNK, (c + 1) * FF_CHUNK)
        g = jnp.dot(hb, w1_ref[:, cols], preferred_element_type=F32)
        v = jnp.dot(hb, w3_ref[:, cols], preferred_element_type=F32)
        act[:, cols] = (g * jax.nn.sigmoid(g) * v).astype(BF16)
    f = jnp.dot(act[...], w2_ref[...], preferred_element_type=F32)
    f = f.reshape(groups, SUBLANES, D_MODEL)
    ms = jnp.mean(f * f, axis=-1, keepdims=True)
    return x + f * lax.rsqrt(ms + EPS) * post_gain


def _store_natural(res, o_ref, il_scr, group0):
    groups = res.shape[0]
    assert SEG_LEN % groups == 0
    res2 = res.reshape(groups * SUBLANES, D_MODEL)
    tile, j0 = divmod(group0, SEG_LEN)
    for c in range(D_MODEL // LANES):
        il_scr[c] = res2[:, c * LANES:(c + 1) * LANES]
    for seg in range(SUBLANES):
        r0 = tile * SUBLANES * SEG_LEN + seg * SEG_LEN + j0
        for c in range(D_MODEL // LANES):
            o_ref[r0:r0 + groups, c * LANES:(c + 1) * LANES] = (
                il_scr[c, pl.ds(seg, groups, stride=SUBLANES), :])


def _load_natural(x_ref, il_scr, group0, groups):
    tile, j0 = divmod(group0, SEG_LEN)
    for seg in range(SUBLANES):
        r0 = tile * SUBLANES * SEG_LEN + seg * SEG_LEN + j0
        for c in range(D_MODEL // LANES):
            il_scr[c, pl.ds(seg, groups, stride=SUBLANES), :] = (
                x_ref[r0:r0 + groups, c * LANES:(c + 1) * LANES])
    x = jnp.concatenate([il_scr[c] for c in range(D_MODEL // LANES)], axis=-1)
    return x.reshape(groups, SUBLANES, D_MODEL)


def _ffn_kernel(*refs, n_next, ada_next, natural_in, natural_out):
    (x_ref, shift_ref, scale_ref, gate_ref, xs_ref, gpre_ref, gpost_ref,
     w1_ref, w3_ref, w2_ref) = refs[:10]
    refs = refs[10:]
    next_src, refs = refs[:n_next], refs[n_next:]
    ada_src, refs = refs[:3 * ada_next], refs[3 * ada_next:]
    (o_ref, os_ref), refs = refs[:2], refs[2:]
    next_dst, refs = refs[:n_next], refs[n_next:]
    ada_dst, refs = refs[:ada_next], refs[ada_next:]
    act_scr, refs = refs[0], refs[1:]
    il_scr = refs[0] if (natural_in or natural_out) else None
    _cast_next_weights(next_src, next_dst)
    if ada_next:
        _ada_step(*ada_src, *ada_dst)
    prompt, streams = slice(0, 1), slice(SUBLANES, 2 * SUBLANES)
    rows_fn = functools.partial(_ffn_rows, w1_ref=w1_ref, w3_ref=w3_ref, w2_ref=w2_ref)

    @pl.when(pl.program_id(0) == 0)
    def _():
        rows_s = xs_ref.shape[0] * SUBLANES
        os_ref[...] = rows_fn(xs_ref[...], gpre_ref[...] * (1.0 + scale_ref[streams, :]),
                              shift_ref[streams, :],
                              0.5 * gate_ref[streams, :] * gpost_ref[...],
                              act=act_scr.at[pl.ds(0, rows_s)])

    tl = o_ref.shape[0] // SUBLANES if natural_out else o_ref.shape[0]
    sub_tl = min(tl, FFN_SUB_GROUPS)
    sub_rows = sub_tl * SUBLANES
    pre_gain = gpre_ref[...] * (1.0 + scale_ref[prompt, :])
    post_gain = 0.5 * gate_ref[prompt, :] * gpost_ref[...]
    shift = shift_ref[prompt, :]
    for s in range(tl // sub_tl):
        grp = slice(s * sub_tl, (s + 1) * sub_tl)
        x = _load_natural(x_ref, il_scr, s * sub_tl, sub_tl) if natural_in else x_ref[grp]
        res = rows_fn(x, pre_gain, shift, post_gain,
                      act=act_scr.at[pl.ds(s * sub_rows, sub_rows)])
        if natural_out:
            _store_natural(res, o_ref, il_scr, s * sub_tl)
        else:
            o_ref[grp] = res


def _build_halo(e_ref, prev_ref, taps, seg_len, chained):
    n = (taps - 1) * SUBLANES
    prev = prev_ref[...]
    if chained:
        cur = e_ref[seg_len * SUBLANES:seg_len * SUBLANES + n, :]
        sub = lax.broadcasted_iota(jnp.int32, cur.shape, 0) & (SUBLANES - 1)
        halo = jnp.where(sub == 0, pltpu.roll(prev, n - (SUBLANES - 1), axis=0),
                         pltpu.roll(cur, 1, axis=0))
    else:
        halo = prev
    e_ref[0:n, :] = halo


def _short_conv(e_ref, w_ref, taps, rows):
    acc = e_ref[0:rows, :] * w_ref[0:1, :]
    for k in range(1, taps):
        acc = acc + e_ref[k * SUBLANES:k * SUBLANES + rows, :] * w_ref[k:k + 1, :]
    return acc


def _stage_in_steps(x, u_dst, pre_gain, shift, win_ref):
    rows = x.shape[0] * SUBLANES
    ms = jnp.mean(x * x, axis=-1, keepdims=True)
    h = x * lax.rsqrt(ms + EPS) * pre_gain + shift
    hb = h.reshape(rows, D_MODEL).astype(BF16)
    for c in range(D_IN // MXU_DIM):
        cols = slice(c * MXU_DIM, (c + 1) * MXU_DIM)
        u_dst[:, cols] = jnp.dot(hb, win_ref[:, cols], preferred_element_type=F32)
        yield


def _stage_in(*args, **kwargs):
    for _ in _stage_in_steps(*args, **kwargs):
        pass


def _no_tick(n=1):
    del n


def _stage_mix(u, x_view, o_view, post_gain, carry, chained, tick, wout_ref, wca_ref, wcb_ref,
               bcb_ref, wr_ref, br_ref, wi_ref, bi_ref, lam_ref, wcc_ref, bcc_ref, lng_ref,
               lnb_ref, grp_ref, ea, eb, ec, hl_scr, p_scr, y_scr):
    na_ref, nb_ref, nh_ref, nc_ref = carry
    seg_len = x_view.shape[0]
    rows = seg_len * SUBLANES

    tick(2)
    na = (CONV_A - 1) * SUBLANES
    ea[na:na + rows, :] = u[:, O_AC:O_AC + D_A] * u[:, O_AX:O_AX + D_A]
    _build_halo(ea, na_ref, CONV_A, seg_len, chained)
    ya = u[:, O_AB:O_AB + D_A] * _short_conv(ea, wca_ref, CONV_A, rows)
    na_ref[...] = ea[rows:rows + na, :]
    y_scr[0:rows, 0:D_A] = _rms(ya, grp_ref[:, 0:D_A]).astype(BF16)

    tick()
    nb = (CONV_B - 1) * SUBLANES
    eb[nb:nb + rows, :] = u[:, O_BX:O_BX + D_B]
    _build_halo(eb, nb_ref, CONV_B, seg_len, chained)
    xb = _short_conv(eb, wcb_ref, CONV_B, rows) + bcb_ref[...]
    nb_ref[...] = eb[rows:rows + nb, :]
    xbb = xb.astype(BF16)
    half = D_B // 2
    gr = jnp.concatenate(
        [jnp.dot(xbb[:, :half], wr_ref[0], preferred_element_type=F32),
         jnp.dot(xbb[:, half:], wr_ref[1], preferred_element_type=F32)], axis=-1)
    gi = jnp.concatenate(
        [jnp.dot(xbb[:, :half], wi_ref[0], preferred_element_type=F32),
         jnp.dot(xbb[:, half:], wi_ref[1], preferred_element_type=F32)], axis=-1)
    tick()
    r = jax.nn.sigmoid(gr + br_ref[...])
    ig = jax.nn.sigmoid(gi + bi_ref[...])
    log_a = RG_C * r * jax.nn.log_sigmoid(lam_ref[...])
    a = jnp.exp(log_a)
    b = jnp.sqrt(-jnp.tanh(log_a) * (a * a + 1.0)) * ig * xb

    tick()
    hl = jnp.zeros((SUBLANES, D_B), F32)
    p = jnp.ones((SUBLANES, D_B), F32)
    for j in range(seg_len):
        grp_rows = slice(j * SUBLANES, (j + 1) * SUBLANES)
        aj = a[grp_rows, :]
        hl = aj * hl + b[grp_rows, :]
        p = aj * p
        hl_scr[grp_rows, :] = hl
        p_scr[grp_rows, :] = p

    prev_end = nh_ref[...]
    if chained:
        sub = lax.broadcasted_iota(jnp.int32, (SUBLANES, D_B), 0)
        ca, cb = p, hl
        for d in (1, 2, 4):
            a_sh = jnp.where(sub >= d, pltpu.roll(ca, d, axis=0), 1.0)
            b_sh = jnp.where(sub >= d, pltpu.roll(cb, d, axis=0), 0.0)
            cb = ca * b_sh + cb
            ca = ca * a_sh
        c0 = jnp.broadcast_to(prev_end[SUBLANES - 1:SUBLANES, :], (SUBLANES, D_B))
        seg_end = ca * c0 + cb
        seg_start = jnp.where(sub == 0, c0, pltpu.roll(seg_end, 1, axis=0))
    else:
        seg_start = prev_end
        seg_end = p * seg_start + hl
    nh_ref[...] = seg_end

    tick()
    hl3 = hl_scr[0:rows, :].reshape(seg_len, SUBLANES, D_B)
    p3 = p_scr[0:rows, :].reshape(seg_len, SUBLANES, D_B)
    hseq = (hl3 + p3 * seg_start).reshape(rows, D_B)
    yb = hseq * jax.nn.gelu(u[:, O_BG:O_BG + D_B])
    y_scr[0:rows, D_A:D_A + D_B] = _rms(yb, grp_ref[:, D_A:D_A + D_B]).astype(BF16)

    nc = (CONV_C - 1) * SUBLANES
    ec[nc:nc + rows, :] = u[:, O_CV:O_CV + D_C] * jax.nn.sigmoid(u[:, O_CG:O_CG + D_C])
    _build_halo(ec, nc_ref, CONV_C, seg_len, chained)
    for blk in range(rows // CONV_ROWS):
        r0 = blk * CONV_ROWS
        acc = ec[r0:r0 + CONV_ROWS, :] * wcc_ref[0:1, :]
        for k in range(1, CONV_C):
            acc = acc + ec[r0 + k * SUBLANES:r0 + k * SUBLANES + CONV_ROWS, :] * wcc_ref[k:k + 1, :]
        yc = acc + bcc_ref[...]
        mu = jnp.mean(yc, axis=-1, keepdims=True)
        var = jnp.mean(jnp.square(yc - mu), axis=-1, keepdims=True)
        yc = (yc - mu) * lax.rsqrt(var + EPS) * lng_ref[...] + lnb_ref[...]
        yc = yc * jax.nn.sigmoid(yc)
        y_scr[r0:r0 + CONV_ROWS, D_A + D_B:] = _rms(yc, grp_ref[:, D_A + D_B:]).astype(BF16)
        if blk % 2 == 1:
            tick()
    nc_ref[...] = ec[rows:rows + nc, :]

    tick(D_IN // MXU_DIM)
    m = jnp.dot(y_scr[0:rows, :], wout_ref[...], preferred_element_type=F32)
    m = m.reshape(seg_len, SUBLANES, D_MODEL)
    ms = jnp.mean(m * m, axis=-1, keepdims=True)
    o_view[...] = x_view[...] + m * lax.rsqrt(ms + EPS) * post_gain


N_MIX_IN = 31


def _mixer_kernel(*refs, n_next):
    (x_ref, xn_ref, shift_ref, scale_ref, gate_ref,
     xs_ref, gpre_ref, gpost_ref, win_ref, wout_ref,
     wca_ref, wcb_ref, bcb_ref, wr_ref, br_ref, wi_ref, bi_ref, lam_ref,
     wcc_ref, bcc_ref, lng_ref, lnb_ref, grp_ref) = refs[:23]
    state_p = refs[23:27]
    state_s = refs[27:N_MIX_IN]
    next_src = refs[N_MIX_IN:N_MIX_IN + n_next]
    outs = refs[N_MIX_IN + n_next:]
    o_ref, carry_p, os_ref, carry_s = outs[0], outs[1:5], outs[5], outs[6:10]
    next_dst = outs[10:10 + n_next]
    u_scr, ea, eb, ec, hl_scr, p_scr, y_scr = outs[10 + n_next:]
    _cast_next_weights(next_src, next_dst)

    seg_len = x_ref.shape[0] // 2
    prompt, streams = slice(0, 1), slice(SUBLANES, 2 * SUBLANES)
    pre_gain = gpre_ref[...] * (1.0 + scale_ref[prompt, :])
    post_gain = gate_ref[prompt, :] * gpost_ref[...]
    shift = shift_ref[prompt, :]
    stage_in = functools.partial(_stage_in, win_ref=win_ref)
    stage_mix = functools.partial(
        _stage_mix, wout_ref=wout_ref, wca_ref=wca_ref, wcb_ref=wcb_ref, bcb_ref=bcb_ref,
        wr_ref=wr_ref, br_ref=br_ref, wi_ref=wi_ref, bi_ref=bi_ref, lam_ref=lam_ref,
        wcc_ref=wcc_ref, bcc_ref=bcc_ref, lng_ref=lng_ref, lnb_ref=lnb_ref, grp_ref=grp_ref,
        ea=ea, eb=eb, ec=ec, hl_scr=hl_scr, p_scr=p_scr, y_scr=y_scr)

    @pl.when(pl.program_id(0) == 0)
    def _():
        for dst, src in zip(carry_p + carry_s, state_p + state_s, strict=True):
            dst[...] = src[...]
        us = u_scr.at[1, pl.ds(0, xs_ref.shape[0] * SUBLANES)]
        stage_in(xs_ref[...], us, gpre_ref[...] * (1.0 + scale_ref[streams, :]),
                 shift_ref[streams, :])
        stage_mix(us, xs_ref, os_ref, gate_ref[streams, :] * gpost_ref[...], carry_s, False,
                  _no_tick)
        stage_in(x_ref[0:seg_len], u_scr.at[0], pre_gain, shift)

    def ticker(x_next, u_dst):
        steps = _stage_in_steps(x_next, u_dst, pre_gain, shift, win_ref)

        def tick(n=1):
            for _ in range(n):
                next(steps, None)
        return tick

    first, second = pl.ds(0, seg_len), pl.ds(seg_len, seg_len)
    stage_mix(u_scr.at[0], x_ref.at[first], o_ref.at[first], post_gain, carry_p, True,
              ticker(x_ref[seg_len:2 * seg_len], u_scr.at[1]))
    stage_mix(u_scr.at[1], x_ref.at[second], o_ref.at[second], post_gain, carry_p, True,
              ticker(xn_ref[...], u_scr.at[0]))


def _const_spec(shape, index):
    return pl.BlockSpec(shape, lambda i: index, pipeline_mode=pl.Buffered(1))


def _whole_out_spec(shape):
    return pl.BlockSpec(shape, lambda i: (0,) * len(shape))


def _mod_specs(sub):
    return [_const_spec((2 * SUBLANES, D_MODEL), (0, 3 * sub + k)) for k in range(3)]


def _ada_specs(w_ada, layer, steps):
    n_mod = w_ada.shape[-1]
    slab = D_MODEL // steps
    in_specs = [pl.BlockSpec((None, 2 * SUBLANES, slab), lambda i: (i, 0, 0)),
                pl.BlockSpec((None, slab, n_mod), lambda i: (layer, i, 0)),
                pl.BlockSpec((None, 1, n_mod), lambda i: (layer, 0, 0))]
    out_spec = pl.BlockSpec((2 * SUBLANES, n_mod), lambda i: (0, 0))
    return in_specs, out_spec, jax.ShapeDtypeStruct((2 * SUBLANES, n_mod), F32)


def _norm_specs(layer, sub):
    return [_const_spec((None, 1, D_MODEL), (layer * N_SUB + sub, 0, 0))] * 2


def _next_weight_specs(next_w, steps):
    in_specs, out_specs, out_shapes = [], [], []
    for arr, lead in next_w:
        r, c = arr.shape[-2:]
        assert r % (steps * 2 * SUBLANES) == 0
        blk = r // steps
        in_specs.append(pl.BlockSpec((None,) * len(lead) + (blk, c),
                                     lambda i, lead=lead: lead + (i, 0)))
        out_specs.append(pl.BlockSpec((blk, c), lambda i: (i, 0)))
        out_shapes.append(jax.ShapeDtypeStruct((r, c), BF16))
    return in_specs, out_specs, out_shapes


def _ffn_call(xp, xs, mod, gpre, gpost, w_bf16, next_w, ada_next, layer, sub,
              natural_in, natural_out):
    frames = xp.shape[0] if natural_in else xp.shape[0] * SUBLANES
    groups = frames // SUBLANES
    tl = FFN_GROUPS
    assert groups % tl == 0 and xs.shape[0] * SUBLANES <= tl * SUBLANES and tl % SEG_LEN == 0
    steps = groups // tl
    il_spec = pl.BlockSpec((tl, SUBLANES, D_MODEL), lambda i: (i, 0, 0))
    nat_spec = pl.BlockSpec((tl * SUBLANES, D_MODEL), lambda i: (i, 0))
    il_shape = jax.ShapeDtypeStruct((groups, SUBLANES, D_MODEL), F32)
    nat_shape = jax.ShapeDtypeStruct((frames, D_MODEL), F32)
    nin, nout, nshape = _next_weight_specs(next_w, steps)
    ain, aout, ashape, aargs = [], [], [], []
    if ada_next is not None:
        c_slabs, w_ada, b_ada = ada_next
        ain, a_out_spec, a_shape = _ada_specs(w_ada, layer + 1, steps)
        aout, ashape, aargs = [a_out_spec], [a_shape], [c_slabs, w_ada, b_ada]
    scratch = [pltpu.VMEM((tl * SUBLANES, D_FF), BF16)]
    if natural_in or natural_out:
        scratch.append(pltpu.VMEM((D_MODEL // LANES, FFN_SUB_GROUPS * SUBLANES, LANES), F32))
    outs = pl.pallas_call(
        functools.partial(_ffn_kernel, n_next=len(next_w), ada_next=len(aout),
                          natural_in=natural_in, natural_out=natural_out),
        grid=(steps,),
        in_specs=[nat_spec if natural_in else il_spec] + _mod_specs(sub)
        + [_const_spec(xs.shape, (0, 0, 0))] + _norm_specs(layer, sub)
        + [_const_spec((D_MODEL, D_FF), (0, 0)), _const_spec((D_MODEL, D_FF), (0, 0)),
           _const_spec((D_FF, D_MODEL), (0, 0))] + nin + ain,
        out_specs=[nat_spec if natural_out else il_spec, _whole_out_spec(xs.shape)] + nout + aout,
        out_shape=[nat_shape if natural_out else il_shape, jax.ShapeDtypeStruct(xs.shape, F32)]
        + nshape + ashape,
        scratch_shapes=scratch,
        compiler_params=pltpu.CompilerParams(
            dimension_semantics=("arbitrary",), vmem_limit_bytes=VMEM_LIMIT),
        name=f"ffn_l{layer}_s{sub}",
    )(xp, mod, mod, mod, xs, gpre, gpost, *w_bf16, *[arr for arr, _ in next_w], *aargs)
    n_w = len(next_w)
    return outs[0], outs[1], tuple(outs[2:2 + n_w]), (outs[2 + n_w] if aout else None)


def _mixer_call(xp, xs, mod, gpre, gpost, w_bf16, next_w, p, states_p, states_s, layer):
    seg_len = SEG_LEN
    n_tiles = xp.shape[0] // seg_len
    assert n_tiles % 2 == 0 and xs.shape[0] <= seg_len
    steps = n_tiles // 2
    rows = seg_len * SUBLANES
    sub = 1
    x_spec = pl.BlockSpec((2 * seg_len, SUBLANES, D_MODEL), lambda i: (i, 0, 0))
    next_spec = pl.BlockSpec((seg_len, SUBLANES, D_MODEL),
                             lambda i: (jnp.minimum(2 * i + 2, n_tiles - 1), 0, 0))

    def lspec(arr):
        shape = (None,) + arr.shape[1:]
        return _const_spec(shape, (layer,) + (0,) * (arr.ndim - 1))

    w_in, w_out = w_bf16
    weights = [p["w_conv_a"], p["w_conv_b"], p["b_conv_b"],
               p["w_gate_r"], p["b_gate_r"], p["w_gate_i"], p["b_gate_i"], p["rg_lambda"],
               p["w_conv_c"], p["b_conv_c"], p["ln_c_g"], p["ln_c_b"], p["grp_g"]]
    states = list(states_p) + list(states_s)
    nin, nout, nshape = _next_weight_specs(next_w, steps)
    in_specs = ([x_spec, next_spec] + _mod_specs(sub)
                + [_const_spec(xs.shape, (0, 0, 0))] + _norm_specs(layer, sub)
                + [_const_spec(w_in.shape, (0, 0)), _const_spec(w_out.shape, (0, 0))]
                + [lspec(w) for w in weights] + [lspec(s) for s in states])
    assert len(in_specs) == N_MIX_IN
    outs = pl.pallas_call(
        functools.partial(_mixer_kernel, n_next=len(next_w)),
        grid=(steps,),
        in_specs=in_specs + nin,
        out_specs=[x_spec] + [_whole_out_spec(s.shape[1:]) for s in states_p]
        + [_whole_out_spec(xs.shape)] + [_whole_out_spec(s.shape[1:]) for s in states_s] + nout,
        out_shape=[jax.ShapeDtypeStruct(xp.shape, F32)]
        + [jax.ShapeDtypeStruct(s.shape[1:], F32) for s in states_p]
        + [jax.ShapeDtypeStruct(xs.shape, F32)]
        + [jax.ShapeDtypeStruct(s.shape[1:], F32) for s in states_s] + nshape,
        scratch_shapes=[
            pltpu.VMEM((2, rows, D_IN), F32),
            pltpu.VMEM((rows + (CONV_A - 1) * SUBLANES, D_A), F32),
            pltpu.VMEM((rows + (CONV_B - 1) * SUBLANES, D_B), F32),
            pltpu.VMEM((rows + (CONV_C - 1) * SUBLANES, D_C), F32),
            pltpu.VMEM((rows, D_B), F32),
            pltpu.VMEM((rows, D_B), F32),
            pltpu.VMEM((rows, D_MIX), BF16),
        ],
        compiler_params=pltpu.CompilerParams(
            dimension_semantics=("arbitrary",), vmem_limit_bytes=VMEM_LIMIT),
        name=f"mixer_l{layer}",
    )(xp, xp, mod, mod, mod, xs, gpre, gpost, w_in, w_out,
      *weights, *states, *[arr for arr, _ in next_w])
    return outs[0], outs[5], outs[1:5], outs[6:10], tuple(outs[10:])


def _block_diag_halves(w):
    heads_per_half = MXU_DIM // HEAD_DIM_B
    w = w.reshape(DEPTH, 2, heads_per_half, HEAD_DIM_B, HEAD_DIM_B)
    eye = jnp.eye(heads_per_half, dtype=w.dtype)
    bd = jnp.einsum("lhaij,ab->lhaibj", w, eye)
    return bd.reshape(DEPTH, 2, MXU_DIM, MXU_DIM).astype(BF16)


def _trunk(xp, xs, mod0, ada, params, states_p, states_s, w_f32):
    def f32_weights(key):
        if key[0] == "ffn":
            lead = key[1:]
            return [(w_f32["ffn_w1"], lead), (w_f32["ffn_w3"], lead), (w_f32["ffn_w2"], lead)]
        return [(w_f32["w_in"], key[1:]), (w_f32["w_out"], key[1:])]

    def cast_targets(key):
        if key[0] == "mix":
            return []
        if key[2] == 0:
            return [("mix", key[1]), ("ffn", key[1], 1)]
        return [("ffn", key[1] + 1, 0)] if key[1] + 1 < DEPTH else []

    order = []
    for l in range(DEPTH):
        order += [("ffn", l, 0), ("mix", l), ("ffn", l, 1)]
    w_bf16 = {order[0]: tuple(arr[lead].astype(BF16) for arr, lead in f32_weights(order[0]))}
    new_p, new_s = [], []
    gpre, gpost = params["norm_pre"], params["norm_post"]
    mods = {0: mod0}
    for key in order:
        targets = cast_targets(key)
        next_w = [w for t in targets for w in f32_weights(t)]
        l = key[1]
        if key[0] == "ffn":
            ada_next = ada if (key[2] == 1 and l + 1 < DEPTH) else None
            xp, xs, cast, mod_next = _ffn_call(
                xp, xs, mods[l], gpre, gpost, w_bf16[key], next_w, ada_next, l, 2 * key[2],
                natural_in=key == order[0], natural_out=key == order[-1])
            if ada_next is not None:
                mods[l + 1] = mod_next
        else:
            xp, xs, ns_p, ns_s, cast = _mixer_call(xp, xs, mods[l], gpre, gpost, w_bf16[key],
                                                   next_w, params, states_p, states_s, l)
            new_p.append(ns_p)
            new_s.append(ns_s)
        for t in targets:
            n_w = len(f32_weights(t))
            w_bf16[t], cast = cast[:n_w], cast[n_w:]
    stack = lambda per_layer: [jnp.stack([ns[k] for ns in per_layer]) for k in range(4)]
    return xp, xs, stack(new_p), stack(new_s)


def kernel(x_prompt, x_sample, state_conv_a, state_conv_b, state_rglru, state_conv_c, c_prompt, c_sample, w_ada, b_ada, norm_pre, norm_post, ffn_w1, ffn_w3, ffn_w2, w_in, w_out, w_conv_a, w_conv_b, b_conv_b, w_gate_r, b_gate_r, w_gate_i, b_gate_i, rg_lambda, w_conv_c, b_conv_c, ln_c_g, ln_c_b, grp_g):
    batch, seq, _ = x_prompt.shape
    dec_batch, dec_seq, _ = x_sample.shape
    assert batch == 1 and dec_batch == SUBLANES
    assert seq % (SUBLANES * SEG_LEN) == 0 and SEG_LEN >= CONV_C - 1
    assert (dec_seq * SUBLANES) % CONV_ROWS == 0

    n_mod = 3 * N_SUB * D_MODEL
    c_all = jnp.concatenate(
        [c_prompt, jnp.zeros((SUBLANES - batch, D_MODEL), F32), c_sample], axis=0)

    def c_slabs(steps):
        return c_all.reshape(2 * SUBLANES, steps, D_MODEL // steps).transpose(1, 0, 2)

    b_ada3 = b_ada.reshape(DEPTH, 1, n_mod)
    ada_steps = D_MODEL // ADA_K
    ain, aout, ashape = _ada_specs(w_ada, 0, ada_steps)
    mod0 = pl.pallas_call(
        _ada_step,
        grid=(ada_steps,),
        in_specs=ain, out_specs=aout, out_shape=ashape,
        compiler_params=pltpu.CompilerParams(
            dimension_semantics=("arbitrary",), vmem_limit_bytes=VMEM_LIMIT),
        name="adaln",
    )(c_slabs(ada_steps), w_ada, b_ada3)

    def row(v):
        return v.reshape(DEPTH, 1, v.shape[-1])

    params = {
        "norm_pre": norm_pre.reshape(DEPTH * N_SUB, 1, D_MODEL),
        "norm_post": norm_post.reshape(DEPTH * N_SUB, 1, D_MODEL),
        "w_conv_a": w_conv_a, "w_conv_b": w_conv_b, "b_conv_b": row(b_conv_b),
        "w_gate_r": _block_diag_halves(w_gate_r), "b_gate_r": row(b_gate_r),
        "w_gate_i": _block_diag_halves(w_gate_i), "b_gate_i": row(b_gate_i),
        "rg_lambda": row(rg_lambda),
        "w_conv_c": w_conv_c, "b_conv_c": row(b_conv_c),
        "ln_c_g": row(ln_c_g), "ln_c_b": row(ln_c_b), "grp_g": row(grp_g),
    }
    w_f32 = {"ffn_w1": ffn_w1, "ffn_w3": ffn_w3, "ffn_w2": ffn_w2, "w_in": w_in, "w_out": w_out}

    xp = x_prompt.reshape(seq, D_MODEL)
    zero_states = [jnp.zeros((DEPTH, (CONV_A - 1) * SUBLANES, D_A), F32),
                   jnp.zeros((DEPTH, (CONV_B - 1) * SUBLANES, D_B), F32),
                   jnp.zeros((DEPTH, SUBLANES, D_B), F32),
                   jnp.zeros((DEPTH, (CONV_C - 1) * SUBLANES, D_C), F32)]

    def to_rows(s):
        return s.transpose(0, 2, 1, 3).reshape(DEPTH, s.shape[2] * SUBLANES, s.shape[3])

    def from_rows(s, taps_m1):
        return s.reshape(DEPTH, taps_m1, SUBLANES, s.shape[-1]).transpose(0, 2, 1, 3)

    xs = x_sample.transpose(1, 0, 2)
    s_states = [to_rows(state_conv_a), to_rows(state_conv_b), state_rglru, to_rows(state_conv_c)]

    ffn_steps = seq // (FFN_GROUPS * SUBLANES)
    yp, ys, (pa, pb, ph, pc), (sa, sb, sh, sc) = _trunk(
        xp, xs, mod0, (c_slabs(ffn_steps), w_ada, b_ada3), params, zero_states, s_states, w_f32)

    y_prompt = yp.reshape(batch, seq, D_MODEL)

    def last_segment(s, taps_m1):
        return s.reshape(DEPTH, taps_m1, SUBLANES, s.shape[-1])[:, :, SUBLANES - 1][:, None]

    pa = last_segment(pa, CONV_A - 1)
    pb = last_segment(pb, CONV_B - 1)
    pc = last_segment(pc, CONV_C - 1)
    ph = ph[:, SUBLANES - 1:SUBLANES]

    y_sample = ys.transpose(1, 0, 2)
    sa = from_rows(sa, CONV_A - 1)
    sb = from_rows(sb, CONV_B - 1)
    sc = from_rows(sc, CONV_C - 1)

    return (y_prompt, y_sample, pa, pb, ph, pc, sa, sb, sh, sc)
```
